```python
import jax, jax.numpy as jnp
from jax import lax
import numpy as np

D_MODEL = 1024
BATCH = 8
SEQ = 4096
DEPTH = 4

CHUNK = 64
MEM_LEN = 256
Q_BLOCK = 128
N_MIXERS = 2
DA_HEADS = 6
DA_HEAD_DIM = 64
DA_WIDTH = DA_HEADS * 2 * DA_HEAD_DIM
HG_HEADS = 6
HG_KEY_DIM = 128
HG_VAL_DIM = 128
HG_KEY_WIDTH = HG_HEADS * HG_KEY_DIM
HG_VAL_WIDTH = HG_HEADS * HG_VAL_DIM
MEM_HEADS = 4
MEM_HEAD_DIM = 64
MEM_WIDTH = MEM_HEADS * MEM_HEAD_DIM
MIX_WIDTH = DA_WIDTH + MEM_WIDTH
D_FF_DENSE = 2816
N_EXPERTS = 8
TOP_K = 2
D_FF_EXPERT = 3584
EPS = 1e-6
MASK_VALUE = -1e30
N_A = (DEPTH + 1) // 2
N_B = DEPTH // 2

kernel_name = "hybrid_diffattn_hgrn2_moe_trunk"

F32 = jnp.float32


def rms_norm(x, g):
    xf = x.astype(F32)
    y = xf * lax.rsqrt(jnp.mean(xf * xf, axis=-1, keepdims=True) + EPS)
    return (y * g.astype(F32)).astype(x.dtype)


def alibi_slopes(n):
    return jnp.asarray(2.0 ** (-8.0 * np.arange(1, n + 1) / n), F32)


def memory_attention(q_mem, mem, mem_norm_g, w_mem_kv):
    B, S, _ = q_mem.shape
    kv = rms_norm(mem, mem_norm_g) @ w_mem_kv
    k, v = jnp.split(kv, 2, axis=-1)
    q = q_mem.reshape(B, S, MEM_HEADS, MEM_HEAD_DIM)
    k = k.reshape(B, -1, MEM_HEADS, MEM_HEAD_DIM)
    v = v.reshape(B, -1, MEM_HEADS, MEM_HEAD_DIM)
    s = jnp.einsum('bshd,bmhd->bhsm', q, k).astype(F32) * (MEM_HEAD_DIM ** -0.5)
    p = jax.nn.softmax(s, axis=-1).astype(v.dtype)
    o = jnp.einsum('bhsm,bmhd->bshd', p, v)
    return o.reshape(B, S, MEM_WIDTH)


def diff_attention(q, k, v, lam, lam_init, subln_g):
    B, S = q.shape[:2]
    nb = S // Q_BLOCK
    scale = DA_HEAD_DIM ** -0.5
    slopes = alibi_slopes(DA_HEADS)
    kpos = jnp.arange(S)
    kchunk = kpos // CHUNK
    qb = q.reshape(B, nb, Q_BLOCK, DA_HEADS, 2, DA_HEAD_DIM).transpose(1, 0, 2, 3, 4, 5)
    starts = jnp.arange(nb) * Q_BLOCK

    def block(args):
        q_blk, start = args
        qpos = start + jnp.arange(Q_BLOCK)
        dist = jnp.abs(qpos[:, None] - kpos[None, :]).astype(F32)
        allowed = kchunk[None, :] <= (qpos // CHUNK)[:, None]
        bias = jnp.where(allowed[None], -slopes[:, None, None] * dist[None], MASK_VALUE)
        s = jnp.einsum('bqhmd,bkhmd->bhmqk', q_blk, k).astype(F32) * scale + bias[None, :, None]
        p = jax.nn.softmax(s, axis=-1)
        a = (p[:, :, 0] - lam * p[:, :, 1]).astype(v.dtype)
        return jnp.einsum('bhqk,bkhe->bqhe', a, v)

    o = lax.map(block, (qb, starts))
    o = o.transpose(1, 0, 2, 3, 4).reshape(B, S, DA_HEADS, 2 * DA_HEAD_DIM)
    o = rms_norm(o, subln_g) * (1.0 - lam_init)
    return o.reshape(B, S, DA_WIDTH)


def diff_attention_mixer(h, mem, w_in, lq1, lk1, lq2, lk2, subln_g, mem_norm_g, w_mem_kv, w_out, lam_init):
    B, S, _ = h.shape
    proj = h @ w_in
    q, k, v, qm = jnp.split(proj, [DA_WIDTH, 2 * DA_WIDTH, 3 * DA_WIDTH], axis=-1)
    q = q.reshape(B, S, DA_HEADS, 2, DA_HEAD_DIM)
    k = k.reshape(B, S, DA_HEADS, 2, DA_HEAD_DIM)
    v = v.reshape(B, S, DA_HEADS, 2 * DA_HEAD_DIM)
    lam = (jnp.exp(jnp.sum(lq1.astype(F32) * lk1.astype(F32)))
           - jnp.exp(jnp.sum(lq2.astype(F32) * lk2.astype(F32))) + lam_init)
    o = diff_attention(q, k, v, lam, lam_init, subln_g).astype(h.dtype)
    om = memory_attention(qm, mem, mem_norm_g, w_mem_kv).astype(h.dtype)
    return jnp.concatenate([o, om], axis=-1) @ w_out


def hgrn2_chunkwise(q, k, v, log_f):
    B, S, H, dk = q.shape
    dv = v.shape[-1]
    nc = S // CHUNK

    def to_chunks(t):
        return t.astype(F32).reshape(B, nc, CHUNK, H, t.shape[-1]).transpose(1, 0, 3, 2, 4)

    xs = (to_chunks(q), to_chunks(k), to_chunks(v), to_chunks(log_f))
    causal = jnp.tril(jnp.ones((CHUNK, CHUNK), bool))[None, None, :, :, None]

    def step(state, inp):
        q_, k_, v_, g_ = inp
        b = jnp.cumsum(g_, axis=2)
        b_last = b[:, :, -1:, :]
        diff = jnp.where(causal, b[:, :, :, None, :] - b[:, :, None, :, :], 0.0)
        decay = jnp.where(causal, jnp.exp(diff), 0.0)
        attn = jnp.einsum('bhtsk,bhsk->bhts', q_[:, :, :, None, :] * decay, k_)
        o = (jnp.einsum('bhts,bhsv->bhtv', attn, v_)
             + jnp.einsum('bhtk,bhkv->bhtv', q_ * jnp.exp(b), state))
        state = (jnp.exp(b_last)[:, :, 0, :, None] * state
                 + jnp.einsum('bhsk,bhsv->bhkv', k_ * jnp.exp(b_last - b), v_))
        return state, o

    state0 = jnp.zeros((B, H, dk, dv), F32)
    _, o = lax.scan(step, state0, xs)
    return o.transpose(1, 0, 3, 2, 4).reshape(B, S, H, dv)


def hgrn2_mixer(h, mem, w_in, lb, out_norm_g, mem_norm_g, w_mem_kv, w_out):
    B, S, _ = h.shape
    proj = h @ w_in
    q, fz, i, g, qm = jnp.split(
        proj, [HG_KEY_WIDTH, 2 * HG_KEY_WIDTH, 2 * HG_KEY_WIDTH + HG_VAL_WIDTH,
               2 * HG_KEY_WIDTH + 2 * HG_VAL_WIDTH], axis=-1)
    fz32 = fz.astype(F32)
    f = lb + (1.0 - lb) * jax.nn.sigmoid(fz32)
    log_f = jnp.log(f)
    k = (1.0 - lb) * jax.nn.sigmoid(-fz32)
    heads = lambda t: t.reshape(B, S, HG_HEADS, -1)
    o = hgrn2_chunkwise(heads(q), heads(k), heads(i), heads(log_f))
    o = rms_norm(o, out_norm_g) * jax.nn.silu(heads(g).astype(F32))
    o = o.reshape(B, S, HG_VAL_WIDTH).astype(h.dtype)
    om = memory_attention(qm, mem, mem_norm_g, w_mem_kv).astype(h.dtype)
    return jnp.concatenate([o, om], axis=-1) @ w_out


def dense_swiglu(h, w_gate_up, w_down):
    gt, up = jnp.split(h @ w_gate_up, 2, axis=-1)
    return (jax.nn.silu(gt) * up) @ w_down


def moe_swiglu(h, w_router, w_gate_up, w_down):
    B, S, D = h.shape
    t = h.reshape(-1, D)
    logits = (t @ w_router).astype(F32)
    top_val, top_idx = lax.top_k(logits, TOP_K)
    top_w = jax.nn.softmax(top_val, axis=-1)
    combine = jnp.sum(jax.nn.one_hot(top_idx, N_EXPERTS, dtype=F32) * top_w[..., None], axis=1)
    out = jnp.zeros_like(t)
    for e in range(N_EXPERTS):
        gt, up = jnp.split(t @ w_gate_up[e], 2, axis=-1)
        y = (jax.nn.silu(gt) * up) @ w_down[e]
        out = out + combine[:, e:e + 1].astype(t.dtype) * y
    return out.reshape(B, S, D)


def setup_inputs(seed: int = 0) -> dict:
    key = jax.random.key(seed)
    ks = iter(jax.random.split(key, 40))
    D = D_MODEL
    res = (2.0 * DEPTH) ** -0.5

    def nrm(shape, scale):
        return jax.random.normal(next(ks), shape, F32) * scale

    def gain(shape):
        return 1.0 + 0.02 * jax.random.normal(next(ks), shape, F32)

    return {
        "x": nrm((BATCH, SEQ, D), 1.0),
        "mem": nrm((BATCH, MEM_LEN, D), 1.0),
        "a_norm_mix": gain((N_A, D)),
        "a_w_in": nrm((N_A, D, 3 * DA_WIDTH + MEM_WIDTH), D ** -0.5),
        "a_lam_q1": nrm((N_A, DA_HEAD_DIM), 0.1),
        "a_lam_k1": nrm((N_A, DA_HEAD_DIM), 0.1),
        "a_lam_q2": nrm((N_A, DA_HEAD_DIM), 0.1),
        "a_lam_k2": nrm((N_A, DA_HEAD_DIM), 0.1),
        "a_subln": gain((N_A, 2 * DA_HEAD_DIM)),
        "a_mem_norm": gain((N_A, D)),
        "a_w_mem_kv": nrm((N_A, D, 2 * MEM_WIDTH), D ** -0.5),
        "a_w_out": nrm((N_A, MIX_WIDTH, D), MIX_WIDTH ** -0.5 * res),
        "b_norm_mix": gain((N_B, D)),
        "b_w_in": nrm((N_B, D, 2 * HG_KEY_WIDTH + 2 * HG_VAL_WIDTH + MEM_WIDTH), D ** -0.5),
        "b_lb_logits": nrm((N_B, HG_KEY_WIDTH), 0.5),
        "b_out_norm": gain((N_B, HG_VAL_DIM)),
        "b_mem_norm": gain((N_B, D)),
        "b_w_mem_kv": nrm((N_B, D, 2 * MEM_WIDTH), D ** -0.5),
        "b_w_out": nrm((N_B, HG_VAL_WIDTH + MEM_WIDTH, D), (HG_VAL_WIDTH + MEM_WIDTH) ** -0.5 * res),
        "dense_norm": gain((N_A, D)),
        "dense_w_gate_up": nrm((N_A, D, 2 * D_FF_DENSE), D ** -0.5),
        "dense_w_down": nrm((N_A, D_FF_DENSE, D), D_FF_DENSE ** -0.5 * res),
        "moe_norm": gain((N_B, D)),
        "moe_router": nrm((N_B, D, N_EXPERTS), D ** -0.5),
        "moe_w_gate_up": nrm((N_B, N_EXPERTS, D, 2 * D_FF_EXPERT), D ** -0.5),
        "moe_w_down": nrm((N_B, N_EXPERTS, D_FF_EXPERT, D), D_FF_EXPERT ** -0.5 * res),
        "final_norm": gain((D,)),
    }


def reference(x, mem,
              a_norm_mix, a_w_in, a_lam_q1, a_lam_k1, a_lam_q2, a_lam_k2, a_subln, a_mem_norm, a_w_mem_kv, a_w_out,
              b_norm_mix, b_w_in, b_lb_logits, b_out_norm, b_mem_norm, b_w_mem_kv, b_w_out,
              dense_norm, dense_w_gate_up, dense_w_down,
              moe_norm, moe_router, moe_w_gate_up, moe_w_down,
              final_norm):
    lb_p = jax.nn.softmax(b_lb_logits.astype(F32), axis=0)
    lb_all = jnp.cumsum(lb_p, axis=0) - lb_p[0:1]
    for i in range(DEPTH):
        if i % N_MIXERS == 0:
            j = i // N_MIXERS
            lam_init = 0.8 - 0.6 * float(np.exp(-0.3 * i))
            h = rms_norm(x, a_norm_mix[j])
            x = x + diff_attention_mixer(h, mem, a_w_in[j], a_lam_q1[j], a_lam_k1[j], a_lam_q2[j], a_lam_k2[j],
                                         a_subln[j], a_mem_norm[j], a_w_mem_kv[j], a_w_out[j], lam_init)
        else:
            j = i // N_MIXERS
            h = rms_norm(x, b_norm_mix[j])
            x = x + hgrn2_mixer(h, mem, b_w_in[j], lb_all[j], b_out_norm[j], b_mem_norm[j],
                                b_w_mem_kv[j], b_w_out[j])
        if i % 2 == 0:
            j = i // 2
            x = x + dense_swiglu(rms_norm(x, dense_norm[j]), dense_w_gate_up[j], dense_w_down[j])
        else:
            j = i // 2
            x = x + moe_swiglu(rms_norm(x, moe_norm[j]), moe_router[j], moe_w_gate_up[j], moe_w_down[j])
    return rms_norm(x, final_norm)
```

```python
import functools

import numpy as np
import jax
import jax.numpy as jnp
from jax import lax
from jax.experimental import pallas as pl
from jax.experimental.pallas import tpu as pltpu

F32 = jnp.float32
BF16 = jnp.bfloat16

DEPTH = 4
N_MIXERS = 2
CHUNK = 64
DA_HEADS = 6
DA_HEAD_DIM = 64
DA_WIDTH = DA_HEADS * 2 * DA_HEAD_DIM
HG_HEADS = 6
HG_KEY_DIM = 128
HG_VAL_DIM = 128
HG_KEY_WIDTH = HG_HEADS * HG_KEY_DIM
HG_VAL_WIDTH = HG_HEADS * HG_VAL_DIM
MEM_HEADS = 4
MEM_HEAD_DIM = 64
MEM_WIDTH = MEM_HEADS * MEM_HEAD_DIM
N_EXPERTS = 8
EPS = 1e-6
MASK_VALUE = -1e30
LOG2E = float(np.log2(np.e))

LANES = 128
SUB_BLOCK = 16
VMEM_LIMIT = 56 * 1024 * 1024


def _params(sem):
    return pltpu.CompilerParams(dimension_semantics=sem, vmem_limit_bytes=VMEM_LIMIT)


def _rms(x, g):
    return x * lax.rsqrt(jnp.mean(x * x, axis=-1, keepdims=True) + EPS) * g


def _dot(a, b):
    return jnp.dot(a, b, preferred_element_type=F32)


def _dot_nt(a, b):
    return lax.dot_general(a, b, (((1,), (1,)), ((), ())), preferred_element_type=F32)


def _memkv_kernel(mem_ref, g_ref, w_ref, out_ref):
    y = _rms(mem_ref[...], g_ref[0]).astype(BF16)
    out_ref[0] = _dot(y, w_ref[0].astype(BF16)).astype(BF16)


def _memkv(mem2d, g, w, mem_len):
    n_layers, d, n = w.shape
    rows = mem2d.shape[0]
    return pl.pallas_call(
        _memkv_kernel,
        out_shape=jax.ShapeDtypeStruct((n_layers, rows, n), BF16),
        grid=(n_layers, rows // mem_len),
        in_specs=[
            pl.BlockSpec((mem_len, d), lambda l, b: (b, 0)),
            pl.BlockSpec((1, 1, d), lambda l, b: (l, 0, 0)),
            pl.BlockSpec((1, d, n), lambda l, b: (l, 0, 0)),
        ],
        out_specs=pl.BlockSpec((1, mem_len, n), lambda l, b: (l, b, 0)),
        compiler_params=_params(("arbitrary", "arbitrary")),
        name="memkv",
    )(mem2d, g.reshape(n_layers, 1, d), w)


NP_CHUNK = 256


def _normproj_kernel(x_ref, g_ref, w_ref, out_ref, *f32_refs, scaled_cols, col_scale, f32_cols):
    xn = _rms(x_ref[...], g_ref[...]).astype(BF16)
    n = w_ref.shape[1]
    for c0 in range(0, n, NP_CHUNK):
        y = _dot(xn, w_ref[:, c0:c0 + NP_CHUNK])
        if c0 < scaled_cols:
            y = y * col_scale
        out_ref[:, c0:c0 + NP_CHUNK] = y.astype(BF16)
        if f32_cols is not None and f32_cols[0] <= c0 < f32_cols[1]:
            f32_refs[0][:, c0 - f32_cols[0]:c0 - f32_cols[0] + NP_CHUNK] = y


def _normproj(x2d, g, w_bf16, *, tm, scaled_cols=0, col_scale=1.0, f32_cols=None):
    t, d = x2d.shape
    n = w_bf16.shape[1]
    assert n % NP_CHUNK == 0 and scaled_cols % NP_CHUNK == 0
    out_shape = [jax.ShapeDtypeStruct((t, n), BF16)]
    out_specs = [pl.BlockSpec((tm, n), lambda i: (i, 0))]
    if f32_cols is not None:
        assert f32_cols[0] % NP_CHUNK == 0 and f32_cols[1] % NP_CHUNK == 0
        wf = f32_cols[1] - f32_cols[0]
        out_shape.append(jax.ShapeDtypeStruct((t, wf), F32))
        out_specs.append(pl.BlockSpec((tm, wf), lambda i: (i, 0)))
    outs = pl.pallas_call(
        functools.partial(_normproj_kernel, scaled_cols=scaled_cols, col_scale=col_scale, f32_cols=f32_cols),
        out_shape=out_shape,
        grid=(t // tm,),
        in_specs=[
            pl.BlockSpec((tm, d), lambda i: (i, 0)),
            pl.BlockSpec((1, d), lambda i: (0, 0)),
            pl.BlockSpec((d, n), lambda i: (0, 0)),
        ],
        out_specs=out_specs,
        compiler_params=_params(("parallel",)),
        name="normproj",
    )(x2d, g.reshape(1, d), w_bf16)
    return outs if f32_cols is not None else outs[0]


def _attn_kernel(dec_ref, q_ref, k_ref, v_ref, rlow_ref, rdiag_ref, lamv_ref, g_ref, o_ref, *, tq, lam_init):
    h = pl.program_id(1)
    qi = pl.program_id(2)
    dec = dec_ref[h]
    q = q_ref[...]
    lane = lax.broadcasted_iota(jnp.int32, q.shape, 1)
    zero = jnp.zeros_like(q)
    qmaps = (jnp.where(lane < DA_HEAD_DIM, q, zero), jnp.where(lane >= DA_HEAD_DIM, q, zero))

    def tile(kstart, bias, carry):
        kt = k_ref[pl.ds(kstart, tq), :]
        vt = v_ref[pl.ds(kstart, tq), :]
        out = []
        for qm, (m, l, a) in zip(qmaps, carry):
            s = _dot_nt(qm, kt) + bias
            mp = m - dec
            mn = jnp.maximum(mp, jnp.max(s, axis=-1, keepdims=True))
            alpha = jnp.exp2(mp - mn)
            p = jnp.exp2(s - mn)
            l = alpha * l + jnp.sum(p, axis=-1, keepdims=True)
            a = alpha * a + _dot(p.astype(BF16), vt)
            out.append((mn, l, a))
        return tuple(out)

    init_one = (jnp.full((tq, 1), MASK_VALUE, F32), jnp.zeros((tq, 1), F32), jnp.zeros((tq, LANES), F32))
    carry = lax.fori_loop(
        0, qi, lambda kj, c: tile(pl.multiple_of(kj * tq, tq), rlow_ref[0], c), (init_one, init_one))
    (_, l1, a1), (_, l2, a2) = tile(pl.multiple_of(qi * tq, tq), rdiag_ref[0], carry)

    lv = lamv_ref[...]
    lam = (jnp.exp(jnp.sum(lv[0:1] * lv[1:2], axis=-1, keepdims=True))
           - jnp.exp(jnp.sum(lv[2:3] * lv[3:4], axis=-1, keepdims=True)) + lam_init)
    o = a1 / l1 - lam * (a2 / l2)
    o_ref[...] = (_rms(o, g_ref[...]) * (1.0 - lam_init)).astype(BF16)


def _alibi_tiles(tq):
    slopes = 2.0 ** (-8.0 * np.arange(1, DA_HEADS + 1) / DA_HEADS)
    r = np.arange(tq)[:, None]
    c = np.arange(tq)[None, :]
    low = slopes[:, None, None] * LOG2E * (c - r)[None]
    diag = np.where((c // CHUNK <= r // CHUNK)[None], -slopes[:, None, None] * LOG2E * np.abs(r - c)[None], MASK_VALUE)
    dec = slopes * LOG2E * tq
    return (jnp.asarray(dec, F32), jnp.asarray(low, F32), jnp.asarray(diag, F32))


def _diff_attention(proj, lamv, subln_g, *, batch, seq, tq, lam_init):
    t = proj.shape[0]
    nq = seq // tq
    dec, low, diag = _alibi_tiles(tq)
    grid_spec = pltpu.PrefetchScalarGridSpec(
        num_scalar_prefetch=1,
        grid=(batch, DA_HEADS, nq),
        in_specs=[
            pl.BlockSpec((tq, LANES), lambda b, h, i, dec: (b * nq + i, h)),
            pl.BlockSpec((seq, LANES), lambda b, h, i, dec: (b, DA_HEADS + h)),
            pl.BlockSpec((seq, LANES), lambda b, h, i, dec: (b, 2 * DA_HEADS + h)),
            pl.BlockSpec((1, tq, tq), lambda b, h, i, dec: (h, 0, 0)),
            pl.BlockSpec((1, tq, tq), lambda b, h, i, dec: (h, 0, 0)),
            pl.BlockSpec((4, DA_HEAD_DIM), lambda b, h, i, dec: (0, 0)),
            pl.BlockSpec((1, LANES), lambda b, h, i, dec: (0, 0)),
        ],
        out_specs=pl.BlockSpec((tq, LANES), lambda b, h, i, dec: (b * nq + i, h)),
    )
    return pl.pallas_call(
        functools.partial(_attn_kernel, tq=tq, lam_init=lam_init),
        out_shape=jax.ShapeDtypeStruct((t, DA_WIDTH), BF16),
        grid_spec=grid_spec,
        compiler_params=_params(("parallel", "parallel", "arbitrary")),
        name="diff_attn",
    )(dec, proj, proj, proj, low, diag, lamv, subln_g.reshape(1, LANES))


def _hgrn_kernel(q_ref, fz_ref, v_ref, gate_ref, lbl_ref, g_ref, o_ref, st_ref, *, layer, n_chunks):
    @pl.when(pl.program_id(2) == 0)
    def _():
        st_ref[...] = jnp.zeros_like(st_ref)

    lbl = lbl_ref[...]
    e = jnp.exp(lbl - jnp.max(lbl, axis=0, keepdims=True))
    pr = e / jnp.sum(e, axis=0, keepdims=True)
    lb = jnp.sum(pr[0:layer + 1], axis=0, keepdims=True) - pr[0:1]

    rr = lax.broadcasted_iota(jnp.int32, (CHUNK, CHUNK), 0)
    cc = lax.broadcasted_iota(jnp.int32, (CHUNK, CHUNK), 1)
    tril = (cc <= rr).astype(F32)
    sub_row = lax.broadcasted_iota(jnp.int32, (SUB_BLOCK, LANES), 0)
    n_sub = CHUNK // SUB_BLOCK

    def chunk(c, _):
        rows = pl.ds(pl.multiple_of(c * CHUNK, CHUNK), CHUNK)
        z = fz_ref[rows, :]
        q = q_ref[rows, :].astype(F32)
        v = v_ref[rows, :].astype(F32)
        f = lb + (1.0 - lb) * (1.0 / (1.0 + jnp.exp(-z)))
        k = (1.0 - lb) * (1.0 / (1.0 + jnp.exp(z)))
        b = jnp.dot(tril, jnp.log(f), preferred_element_type=F32, precision=lax.Precision.HIGHEST)
        vb = v.astype(BF16)
        st = st_ref[...]
        o_inter = _dot_nt((q * jnp.exp(b)).astype(BF16), st.astype(BF16))

        outs = []
        for i in range(n_sub):
            r0 = i * SUB_BLOCK
            qi = q[r0:r0 + SUB_BLOCK]
            ki = k[r0:r0 + SUB_BLOCK]
            bi = b[r0:r0 + SUB_BLOCK]
            vi = v[r0:r0 + SUB_BLOCK]
            oi = jnp.zeros((SUB_BLOCK, LANES), F32)
            for s in range(SUB_BLOCK):
                dcy = jnp.where(sub_row >= s, jnp.exp(bi - bi[s:s + 1]), 0.0)
                a_s = jnp.sum(dcy * qi * ki[s:s + 1], axis=-1, keepdims=True)
                oi = oi + a_s * vi[s:s + 1]
            if i > 0:
                ref_b = b[r0 - 1:r0]
                qd = (qi * jnp.exp(bi - ref_b)).astype(BF16)
                kd = (k[0:r0] * jnp.exp(ref_b - b[0:r0])).astype(BF16)
                oi = oi + _dot(_dot_nt(qd, kd).astype(BF16), vb[0:r0])
            outs.append(oi)
        o = jnp.concatenate(outs, axis=0) + o_inter

        b_last = b[CHUNK - 1:CHUNK]
        kd = (k * jnp.exp(b_last - b)).astype(BF16)
        upd = lax.dot_general(vb, kd, (((0,), (0,)), ((), ())), preferred_element_type=F32)
        st_ref[...] = st * jnp.exp(b_last) + upd

        gt = gate_ref[rows, :].astype(F32)
        o_ref[rows, :] = (_rms(o, g_ref[...]) * (gt * (1.0 / (1.0 + jnp.exp(-gt))))).astype(BF16)
        return 0

    lax.fori_loop(0, n_chunks, chunk, 0)


def _hgrn(proj, fz32, lb_logits, out_norm_g, *, batch, seq, ts, layer):
    t = proj.shape[0]
    ns = seq // ts
    n_b = lb_logits.shape[0]
    return pl.pallas_call(
        functools.partial(_hgrn_kernel, layer=layer, n_chunks=ts // CHUNK),
        out_shape=jax.ShapeDtypeStruct((t, HG_VAL_WIDTH), BF16),
        grid=(batch, HG_HEADS, ns),
        in_specs=[
            pl.BlockSpec((ts, LANES), lambda b, h, i: (b * ns + i, h)),
            pl.BlockSpec((ts, LANES), lambda b, h, i: (b * ns + i, h)),
            pl.BlockSpec((ts, LANES), lambda b, h, i: (b * ns + i, 2 * HG_HEADS + h)),
            pl.BlockSpec((ts, LANES), lambda b, h, i: (b * ns + i, 3 * HG_HEADS + h)),
            pl.BlockSpec((n_b, LANES), lambda b, h, i: (0, h)),
            pl.BlockSpec((1, LANES), lambda b, h, i: (0, 0)),
        ],
        out_specs=pl.BlockSpec((ts, LANES), lambda b, h, i: (b * ns + i, h)),
        scratch_shapes=[pltpu.VMEM((HG_VAL_DIM, HG_KEY_DIM), F32)],
        compiler_params=_params(("parallel", "parallel", "arbitrary")),
        name="hgrn2",
    )(proj, fz32, proj, proj, lb_logits, out_norm_g.reshape(1, LANES))


def _outproj_kernel(o_ref, qm_ref, kv_ref, x_ref, w_ref, out_ref):
    qm = qm_ref[...]
    kmem = kv_ref[0, :, 0:MEM_WIDTH]
    vmem = kv_ref[0, :, MEM_WIDTH:2 * MEM_WIDTH]
    head = lax.broadcasted_iota(jnp.int32, qm.shape, 1) // MEM_HEAD_DIM
    om = jnp.zeros(qm.shape, F32)
    for hh in range(MEM_HEADS):
        qh = jnp.where(head == hh, qm, jnp.zeros_like(qm))
        s = _dot_nt(qh, kmem) * (MEM_HEAD_DIM ** -0.5)
        p = jnp.exp(s - jnp.max(s, axis=-1, keepdims=True))
        p = p / jnp.sum(p, axis=-1, keepdims=True)
        om = jnp.where(head == hh, _dot(p.astype(BF16), vmem), om)
    wo = o_ref.shape[1]
    out_ref[...] = (x_ref[...] + _dot(o_ref[...], w_ref[0:wo, :])
                    + _dot(om.astype(BF16), w_ref[wo:wo + MEM_WIDTH, :]))


def _outproj(o, proj, qm_block, memkv, layer, x2d, w_out_bf16, *, tm, seq, mem_len):
    t, d = x2d.shape
    wo = o.shape[1]
    tiles_per_batch = seq // tm
    return pl.pallas_call(
        _outproj_kernel,
        out_shape=jax.ShapeDtypeStruct((t, d), F32),
        grid=(t // tm,),
        in_specs=[
            pl.BlockSpec((tm, wo), lambda i: (i, 0)),
            pl.BlockSpec((tm, MEM_WIDTH), lambda i: (i, qm_block)),
            pl.BlockSpec((1, mem_len, 2 * MEM_WIDTH), lambda i: (layer, i // tiles_per_batch, 0)),
            pl.BlockSpec((tm, d), lambda i: (i, 0)),
            pl.BlockSpec((wo + MEM_WIDTH, d), lambda i: (0, 0)),
        ],
        out_specs=pl.BlockSpec((tm, d), lambda i: (i, 0)),
        compiler_params=_params(("parallel",)),
        name="outproj",
    )(o, proj, memkv, x2d, w_out_bf16)


def _ffn_kernel(te_ref, nt_ref, x_ref, g_ref, wg_ref, wu_ref, wd_ref, out_ref, xn_ref, acc_ref, *, dense):
    i = pl.program_id(0)
    j = pl.program_id(1)

    @pl.when(i < nt_ref[0])
    def _():
        @pl.when(j == 0)
        def _():
            x = x_ref[...]
            xn_ref[...] = (_rms(x, g_ref[...]) if dense else x).astype(BF16)

        xn = xn_ref[...]
        gt = _dot(xn, wg_ref[0])
        up = _dot(xn, wu_ref[0])
        act = (gt * (1.0 / (1.0 + jnp.exp(-gt))) * up).astype(BF16)
        part = _dot(act, wd_ref[0])

        @pl.when(j == 0)
        def _():
            acc_ref[...] = part

        @pl.when(j > 0)
        def _():
            acc_ref[...] += part

        @pl.when(j == pl.num_programs(1) - 1)
        def _():
            out_ref[...] = acc_ref[...] + x_ref[...] if dense else acc_ref[...]

    @pl.when(jnp.logical_and(i >= nt_ref[0], j == pl.num_programs(1) - 1))
    def _():
        out_ref[...] = jnp.zeros_like(out_ref)


def _ffn(tile_expert, n_tiles, x2d, g, w_gate_up_bf16, w_down_bf16, *, tm, tf, dense):
    r, d = x2d.shape
    f = w_down_bf16.shape[1]
    nf = f // tf
    grid_spec = pltpu.PrefetchScalarGridSpec(
        num_scalar_prefetch=2,
        grid=(r // tm, nf),
        in_specs=[
            pl.BlockSpec((tm, d), lambda i, j, te, nt: (i, 0)),
            pl.BlockSpec((1, d), lambda i, j, te, nt: (0, 0)),
            pl.BlockSpec((1, d, tf), lambda i, j, te, nt: (te[i], 0, j)),
            pl.BlockSpec((1, d, tf), lambda i, j, te, nt: (te[i], 0, nf + j)),
            pl.BlockSpec((1, tf, d), lambda i, j, te, nt: (te[i], j, 0)),
        ],
        out_specs=pl.BlockSpec((tm, d), lambda i, j, te, nt: (i, 0)),
        scratch_shapes=[pltpu.VMEM((tm, d), BF16), pltpu.VMEM((tm, d), F32)],
    )
    return pl.pallas_call(
        functools.partial(_ffn_kernel, dense=dense),
        out_shape=jax.ShapeDtypeStruct((r, d), F32),
        grid_spec=grid_spec,
        compiler_params=_params(("parallel", "arbitrary")),
        name="swiglu_dense" if dense else "swiglu_grouped",
    )(tile_expert, n_tiles, x2d, g.reshape(1, d), w_gate_up_bf16, w_gate_up_bf16, w_down_bf16)


META_E1, META_E2, META_W1, META_W2, META_R1, META_R2 = range(6)


def _router_kernel(x_ref, g_ref, wr_ref, meta_ref, cnt_ref, run_ref):
    @pl.when(pl.program_id(0) == 0)
    def _():
        run_ref[...] = jnp.zeros_like(run_ref)

    tm = x_ref.shape[0]
    xn = _rms(x_ref[...], g_ref[...])
    logits = jnp.dot(xn, wr_ref[...], preferred_element_type=F32, precision=lax.Precision.HIGHEST)
    lane = lax.broadcasted_iota(jnp.int32, logits.shape, 1)
    logits = jnp.where(lane < N_EXPERTS, logits, -jnp.inf)

    def top(vals):
        best = jnp.max(vals, axis=-1, keepdims=True)
        idx = jnp.min(jnp.where(vals == best, lane, LANES), axis=-1, keepdims=True)
        return best, idx

    v1, e1 = top(logits)
    v2, e2 = top(jnp.where(lane == e1, -jnp.inf, logits))
    ex = jnp.exp(v2 - v1)
    w1 = 1.0 / (1.0 + ex)
    w2 = ex / (1.0 + ex)

    oh1 = (lane == e1).astype(F32)
    oh2 = (lane == e2).astype(F32)
    rr = lax.broadcasted_iota(jnp.int32, (tm, tm), 0)
    cc = lax.broadcasted_iota(jnp.int32, (tm, tm), 1)
    strict = (cc < rr).astype(BF16)
    before = _dot(strict, (oh1 + oh2).astype(BF16)) + run_ref[0:1, :]
    r1 = jnp.sum(before * oh1, axis=-1, keepdims=True)
    r2 = jnp.sum(before * oh2, axis=-1, keepdims=True)
    run_ref[0:1, :] = run_ref[0:1, :] + jnp.sum(oh1 + oh2, axis=0, keepdims=True)

    meta = jnp.zeros(logits.shape, F32)
    for slot, val in ((META_E1, e1.astype(F32)), (META_E2, e2.astype(F32)), (META_W1, w1), (META_W2, w2),
                      (META_R1, r1), (META_R2, r2)):
        meta = jnp.where(lane == slot, val, meta)
    meta_ref[...] = meta
    cnt_ref[...] = run_ref[...]


def _router(x2d, g, w_router_padded, *, tm):
    t, d = x2d.shape
    return pl.pallas_call(
        _router_kernel,
        out_shape=[jax.ShapeDtypeStruct((t, LANES), F32), jax.ShapeDtypeStruct((8, LANES), F32)],
        grid=(t // tm,),
        in_specs=[
            pl.BlockSpec((tm, d), lambda i: (i, 0)),
            pl.BlockSpec((1, d), lambda i: (0, 0)),
            pl.BlockSpec((d, LANES), lambda i: (0, 0)),
        ],
        out_specs=[pl.BlockSpec((tm, LANES), lambda i: (i, 0)), pl.BlockSpec((8, LANES), lambda i: (0, 0))],
        scratch_shapes=[pltpu.VMEM((8, LANES), F32)],
        compiler_params=_params(("arbitrary",)),
        name="moe_router",
    )(x2d, g.reshape(1, d), w_router_padded)


def _dispatch_kernel(slot_ref, x_ref, g_ref, zeros_ref, xs_ref, xn_ref, sem):
    del zeros_ref
    tm = x_ref.shape[0]
    xn_ref[...] = _rms(x_ref[...], g_ref[...])

    def row_copy(r, k):
        return pltpu.make_async_copy(xn_ref.at[pl.ds(r, 1)], xs_ref.at[pl.ds(slot_ref[0, 0, 2 * r + k], 1)], sem)

    def start(r, _):
        row_copy(r, 0).start()
        row_copy(r, 1).start()
        return 0

    def wait(r, _):
        row_copy(r, 0).wait()
        row_copy(r, 1).wait()
        return 0

    lax.fori_loop(0, tm, start, 0)
    lax.fori_loop(0, tm, wait, 0)


def _dispatch(slots3, x2d, g, n_slots, *, tm):
    t, d = x2d.shape
    return pl.pallas_call(
        _dispatch_kernel,
        out_shape=jax.ShapeDtypeStruct((n_slots, d), F32),
        grid=(t // tm,),
        in_specs=[
            pl.BlockSpec((1, 1, 2 * tm), lambda i: (i, 0, 0), memory_space=pltpu.SMEM),
            pl.BlockSpec((tm, d), lambda i: (i, 0)),
            pl.BlockSpec((1, d), lambda i: (0, 0)),
            pl.BlockSpec(memory_space=pl.ANY),
        ],
        out_specs=pl.BlockSpec(memory_space=pl.ANY),
        scratch_shapes=[pltpu.VMEM((tm, d), F32), pltpu.SemaphoreType.DMA],
        input_output_aliases={3: 0},
        compiler_params=_params(("arbitrary",)),
        name="moe_dispatch",
    )(slots3, x2d, g.reshape(1, d), jnp.zeros((n_slots, d), F32))


def _combine_kernel(slot_ref, x_ref, meta_ref, y_ref, g_ref, out_ref, ybuf_ref, sem, *, final_norm):
    tm = x_ref.shape[0]

    def row_copy(r, k):
        return pltpu.make_async_copy(y_ref.at[pl.ds(slot_ref[0, 0, 2 * r + k], 1)], ybuf_ref.at[k, pl.ds(r, 1)], sem)

    def start(r, _):
        row_copy(r, 0).start()
        row_copy(r, 1).start()
        return 0

    def wait(r, _):
        row_copy(r, 0).wait()
        row_copy(r, 1).wait()
        return 0

    lax.fori_loop(0, tm, start, 0)
    lax.fori_loop(0, tm, wait, 0)
    meta = meta_ref[...]
    w1 = meta[:, META_W1:META_W1 + 1]
    w2 = meta[:, META_W2:META_W2 + 1]
    out = x_ref[...] + (w1 * ybuf_ref[0] + w2 * ybuf_ref[1])
    out_ref[...] = _rms(out, g_ref[...]) if final_norm else out


def _combine(slots3, x2d, meta, y, g, *, tm, final_norm):
    t, d = x2d.shape
    return pl.pallas_call(
        functools.partial(_combine_kernel, final_norm=final_norm),
        out_shape=jax.ShapeDtypeStruct((t, d), F32),
        grid=(t // tm,),
        in_specs=[
            pl.BlockSpec((1, 1, 2 * tm), lambda i: (i, 0, 0), memory_space=pltpu.SMEM),
            pl.BlockSpec((tm, d), lambda i: (i, 0)),
            pl.BlockSpec((tm, LANES), lambda i: (i, 0)),
            pl.BlockSpec(memory_space=pl.ANY),
            pl.BlockSpec((1, d), lambda i: (0, 0)),
        ],
        out_specs=pl.BlockSpec((tm, d), lambda i: (i, 0)),
        scratch_shapes=[pltpu.VMEM((2, tm, d), F32), pltpu.SemaphoreType.DMA],
        compiler_params=_params(("arbitrary",)),
        name="moe_combine",
    )(slots3, x2d, meta, y, g.reshape(1, d))


def _moe(x2d, norm_g, w_router, w_gate_up, w_down, final_g, *, tm_route, tm_row, tm_ffn, tf, final_norm):
    t, d = x2d.shape
    wr = jnp.zeros((d, LANES), F32).at[:, :N_EXPERTS].set(w_router)
    meta, counts = _router(x2d, norm_g, wr, tm=tm_route)

    cnt = counts[0, :N_EXPERTS].astype(jnp.int32)
    padded = (cnt + tm_ffn - 1) // tm_ffn * tm_ffn
    ends = jnp.cumsum(padded)
    starts = ends - padded
    e12 = meta[:, META_E1:META_E2 + 1].astype(jnp.int32)
    r12 = meta[:, META_R1:META_R2 + 1].astype(jnp.int32)
    slots = starts[e12] + r12
    slots3 = slots.reshape(t // tm_row, 1, 2 * tm_row)
    n_tiles_max = (2 * t) // tm_ffn + N_EXPERTS
    tile_expert = jnp.minimum(
        jnp.searchsorted(ends, jnp.arange(n_tiles_max, dtype=jnp.int32) * tm_ffn, side="right"), N_EXPERTS - 1
    ).astype(jnp.int32)
    n_tiles = (ends[N_EXPERTS - 1] // tm_ffn).astype(jnp.int32).reshape(1)

    xs = _dispatch(slots3, x2d, norm_g, n_tiles_max * tm_ffn, tm=tm_row)
    ys = _ffn(tile_expert, n_tiles, xs, norm_g, w_gate_up.astype(BF16), w_down.astype(BF16),
              tm=tm_ffn, tf=tf, dense=False)
    return _combine(slots3, x2d, meta, ys, final_g, tm=tm_row, final_norm=final_norm)


def _tiles(batch, seq):
    full = seq >= 4096
    return dict(
        tm_proj=512 if full else 128,
        tq=256 if full else 128,
        ts=512 if full else 128,
        tm_out=512 if full else 128,
        tm_dense=512 if full else 128,
        tm_route=512 if full else 128,
        tm_row=256 if full else 128,
        tm_ffn=1024 if full else 128,
    )


def kernel(x, mem, a_norm_mix, a_w_in, a_lam_q1, a_lam_k1, a_lam_q2, a_lam_k2, a_subln, a_mem_norm, a_w_mem_kv, a_w_out, b_norm_mix, b_w_in, b_lb_logits, b_out_norm, b_mem_norm, b_w_mem_kv, b_w_out, dense_norm, dense_w_gate_up, dense_w_down, moe_norm, moe_router, moe_w_gate_up, moe_w_down, final_norm):
    batch, seq, d = x.shape
    mem_len = mem.shape[1]
    t = batch * seq
    ts = _tiles(batch, seq)
    x2d = x.reshape(t, d)
    mem2d = mem.reshape(batch * mem_len, d)
    kv_a = _memkv(mem2d, a_mem_norm, a_w_mem_kv, mem_len)
    kv_b = _memkv(mem2d, b_mem_norm, b_w_mem_kv, mem_len)
    one_tile = jnp.zeros((t // ts["tm_dense"],), jnp.int32)
    all_tiles = jnp.full((1,), t // ts["tm_dense"], jnp.int32)
    d_ff = dense_w_down.shape[1]
    tf_dense = d_ff // 2 if (d_ff // 2) % LANES == 0 else d_ff
    tf_moe = 512 if moe_w_down.shape[2] % 512 == 0 else moe_w_down.shape[2]

    for i in range(DEPTH):
        j = i // N_MIXERS
        if i % N_MIXERS == 0:
            lam_init = 0.8 - 0.6 * float(np.exp(-0.3 * i))
            proj = _normproj(x2d, a_norm_mix[j], a_w_in[j].astype(BF16), tm=ts["tm_proj"],
                             scaled_cols=DA_WIDTH, col_scale=DA_HEAD_DIM ** -0.5 * LOG2E)
            lamv = jnp.stack([a_lam_q1[j], a_lam_k1[j], a_lam_q2[j], a_lam_k2[j]])
            o = _diff_attention(proj, lamv, a_subln[j], batch=batch, seq=seq, tq=ts["tq"], lam_init=lam_init)
            x2d = _outproj(o, proj, 3 * DA_WIDTH // MEM_WIDTH, kv_a, j, x2d, a_w_out[j].astype(BF16),
                           tm=ts["tm_out"], seq=seq, mem_len=mem_len)
        else:
            proj, fz32 = _normproj(x2d, b_norm_mix[j], b_w_in[j].astype(BF16), tm=ts["tm_proj"],
                                   f32_cols=(HG_KEY_WIDTH, 2 * HG_KEY_WIDTH))
            o = _hgrn(proj, fz32, b_lb_logits, b_out_norm[j], batch=batch, seq=seq, ts=ts["ts"], layer=j)
            x2d = _outproj(o, proj, (2 * HG_KEY_WIDTH + 2 * HG_VAL_WIDTH) // MEM_WIDTH, kv_b, j, x2d,
                           b_w_out[j].astype(BF16), tm=ts["tm_out"], seq=seq, mem_len=mem_len)
        if i % 2 == 0:
            x2d = _ffn(one_tile, all_tiles, x2d, dense_norm[j], dense_w_gate_up[j].astype(BF16)[None],
                       dense_w_down[j].astype(BF16)[None], tm=ts["tm_dense"], tf=tf_dense, dense=True)
        else:
            x2d = _moe(x2d, moe_norm[j], moe_router[j], moe_w_gate_up[j], moe_w_down[j], final_norm,
                       tm_route=ts["tm_route"], tm_row=ts["tm_row"], tm_ffn=ts["tm_ffn"], tf=tf_moe,
                       final_norm=(i == DEPTH - 1))
    return x2d.reshape(batch, seq, d)
```

```python
import functools

import numpy as np
import jax
import jax.numpy as jnp
from jax import lax
from jax.experimental import pallas as pl
from jax.experimental.pallas import tpu as pltpu

F32 = jnp.float32
BF16 = jnp.bfloat16

DEPTH = 4
N_MIXERS = 2
CHUNK = 64
DA_HEADS = 6
DA_HEAD_DIM = 64
DA_WIDTH = DA_HEADS * 2 * DA_HEAD_DIM
HG_HEADS = 6
HG_KEY_DIM = 128
HG_VAL_DIM = 128
HG_KEY_WIDTH = HG_HEADS * HG_KEY_DIM
HG_VAL_WIDTH = HG_HEADS * HG_VAL_DIM
MEM_HEADS = 4
MEM_HEAD_DIM = 64
MEM_WIDTH = MEM_HEADS * MEM_HEAD_DIM
N_EXPERTS = 8
EPS = 1e-6
MASK_VALUE = -1e30
LOG2E = float(np.log2(np.e))

LANES = 128
SUB_BLOCK = 16
HG_SAFE_LOG_DECAY = -64.0
VMEM_LIMIT = 56 * 1024 * 1024


def _params(sem):
    return pltpu.CompilerParams(dimension_semantics=sem, vmem_limit_bytes=VMEM_LIMIT)


def _rms(x, g):
    return x * lax.rsqrt(jnp.mean(x * x, axis=-1, keepdims=True) + EPS) * g


def _dot(a, b):
    return jnp.dot(a, b, preferred_element_type=F32)


def _dot_nt(a, b):
    return lax.dot_general(a, b, (((1,), (1,)), ((), ())), preferred_element_type=F32)


def _memkv_kernel(mem_ref, g_ref, w_ref, out_ref):
    y = _rms(mem_ref[...], g_ref[0]).astype(BF16)
    out_ref[0] = _dot(y, w_ref[0].astype(BF16)).astype(BF16)


def _memkv(mem2d, g, w, mem_len):
    n_layers, d, n = w.shape
    rows = mem2d.shape[0]
    return pl.pallas_call(
        _memkv_kernel,
        out_shape=jax.ShapeDtypeStruct((n_layers, rows, n), BF16),
        grid=(n_layers, rows // mem_len),
        in_specs=[
            pl.BlockSpec((mem_len, d), lambda l, b: (b, 0)),
            pl.BlockSpec((1, 1, d), lambda l, b: (l, 0, 0)),
            pl.BlockSpec((1, d, n), lambda l, b: (l, 0, 0)),
        ],
        out_specs=pl.BlockSpec((1, mem_len, n), lambda l, b: (l, b, 0)),
        compiler_params=_params(("arbitrary", "arbitrary")),
        name="memkv",
    )(mem2d, g.reshape(n_layers, 1, d), w)


NP_CHUNK = 256


def _normproj_kernel(x_ref, g_ref, w_ref, out_ref, *f32_refs, scaled_cols, col_scale, f32_cols):
    xn = _rms(x_ref[...], g_ref[...]).astype(BF16)
    n = w_ref.shape[1]
    for c0 in range(0, n, NP_CHUNK):
        y = _dot(xn, w_ref[:, c0:c0 + NP_CHUNK])
        if c0 < scaled_cols:
            y = y * col_scale
        out_ref[:, c0:c0 + NP_CHUNK] = y.astype(BF16)
        if f32_cols is not None and f32_cols[0] <= c0 < f32_cols[1]:
            f32_refs[0][:, c0 - f32_cols[0]:c0 - f32_cols[0] + NP_CHUNK] = y


def _normproj(x2d, g, w_bf16, *, tm, scaled_cols=0, col_scale=1.0, f32_cols=None):
    t, d = x2d.shape
    n = w_bf16.shape[1]
    assert n % NP_CHUNK == 0 and scaled_cols % NP_CHUNK == 0
    out_shape = [jax.ShapeDtypeStruct((t, n), BF16)]
    out_specs = [pl.BlockSpec((tm, n), lambda i: (i, 0))]
    if f32_cols is not None:
        assert f32_cols[0] % NP_CHUNK == 0 and f32_cols[1] % NP_CHUNK == 0
        wf = f32_cols[1] - f32_cols[0]
        out_shape.append(jax.ShapeDtypeStruct((t, wf), F32))
        out_specs.append(pl.BlockSpec((tm, wf), lambda i: (i, 0)))
    outs = pl.pallas_call(
        functools.partial(_normproj_kernel, scaled_cols=scaled_cols, col_scale=col_scale, f32_cols=f32_cols),
        out_shape=out_shape,
        grid=(t // tm,),
        in_specs=[
            pl.BlockSpec((tm, d), lambda i: (i, 0)),
            pl.BlockSpec((1, d), lambda i: (0, 0)),
            pl.BlockSpec((d, n), lambda i: (0, 0)),
        ],
        out_specs=out_specs,
        compiler_params=_params(("parallel",)),
        name="normproj",
    )(x2d, g.reshape(1, d), w_bf16)
    return outs if f32_cols is not None else outs[0]


ATT_ROWS = 32


def _fold8(x, op):
    out = x[0:8]
    for r in range(8, x.shape[0], 8):
        out = op(out, x[r:r + 8])
    return out


def _attn_kernel(dec_ref, q_ref, k_ref, v_ref, bias_ref, lamv_ref, g_ref, o_ref,
                 vt_ref, s0_ref, s1_ref, p0_ref, p1_ref, st0_ref, st1_ref, acc_ref, *, tq, lam_init):
    h = pl.program_id(1)
    nq = q_ref.shape[0] // tq
    n_steps = nq * (nq + 1) // 2 + 2
    assert n_steps % 2 == 0
    s_refs, p_refs, st_refs = (s0_ref, s1_ref), (p0_ref, p1_ref), (st0_ref, st1_ref)

    for j in range(nq):
        vt_ref[j] = v_ref[j * tq:(j + 1) * tq, :].astype(F32).T.astype(BF16)
    for ref in (s0_ref, s1_ref, p0_ref, p1_ref, acc_ref):
        ref[...] = jnp.zeros_like(ref)
    for ref in st_refs:
        ref[...] = jnp.ones_like(ref)

    dec = dec_ref[h]
    lo_map = lax.broadcasted_iota(jnp.int32, (tq, LANES), 1) < DA_HEAD_DIM
    lv = lamv_ref[...]
    lam = (jnp.exp(jnp.sum(lv[0:1] * lv[1:2], axis=-1, keepdims=True))
           - jnp.exp(jnp.sum(lv[2:3] * lv[3:4], axis=-1, keepdims=True)) + lam_init)
    out_gain = g_ref[...] * (1.0 - lam_init)

    def step(slot, pairs):
        (qa, ka), (qb, kb), (qc, kc) = pairs
        s_a, s_b = s_refs[slot], s_refs[1 - slot]
        p_b, p_c = p_refs[1 - slot], p_refs[slot]
        st_b, st_prev = st_refs[1 - slot], st_refs[slot]

        q = q_ref[pl.ds(pl.multiple_of(qa * tq, tq), tq), :]
        kt = k_ref[pl.ds(pl.multiple_of(ka * tq, tq), tq), :]
        zero = jnp.zeros_like(q)
        s_a[0] = _dot_nt(kt, jnp.where(lo_map, q, zero))
        s_a[1] = _dot_nt(kt, jnp.where(lo_map, zero, q))

        diag = (kb == qb).astype(jnp.int32)
        first_b = kb == 0
        mx8 = [None, None]
        for c in range(0, tq, ATT_ROWS):
            bias = bias_ref[0, diag, c:c + ATT_ROWS, :]
            for mi in range(2):
                s = s_b[mi, c:c + ATT_ROWS, :] + bias
                s_b[mi, c:c + ATT_ROWS, :] = s
                cm = _fold8(s, jnp.maximum)
                mx8[mi] = cm if mx8[mi] is None else jnp.maximum(mx8[mi], cm)
        for mi in range(2):
            m_old = jnp.where(first_b, MASK_VALUE, st_prev[mi, 0:1, :])
            l_old = jnp.where(first_b, 0.0, st_prev[mi, 1:2, :])
            mp = m_old - dec
            mn = jnp.maximum(mp, jnp.max(mx8[mi], axis=0, keepdims=True))
            alpha = jnp.exp2(mp - mn)
            l8 = None
            for c in range(0, tq, ATT_ROWS):
                p = jnp.exp2(s_b[mi, c:c + ATT_ROWS, :] - mn)
                p_b[mi, c:c + ATT_ROWS, :] = p.astype(BF16)
                cs = _fold8(p, jnp.add)
                l8 = cs if l8 is None else l8 + cs
            st_b[mi, 0:1, :] = mn
            st_b[mi, 1:2, :] = alpha * l_old + jnp.sum(l8, axis=0, keepdims=True)
            st_b[mi, 2:3, :] = alpha

        vt = vt_ref[kc]
        for mi in range(2):
            kept = jnp.where(kc == 0, 0.0, st_prev[mi, 2:3, :] * acc_ref[mi])
            acc_ref[mi] = kept + _dot(vt, p_c[mi])

        @pl.when(kc == qc)
        def _():
            ot = acc_ref[0] / st_prev[0, 1:2, :] - lam * (acc_ref[1] / st_prev[1, 1:2, :])
            ot = ot * lax.rsqrt(jnp.mean(ot * ot, axis=0, keepdims=True) + EPS) * out_gain
            o_ref[pl.ds(pl.multiple_of(qc * tq, tq), tq), :] = ot.T.astype(BF16)

    def advance(pair):
        qx, kx = pair
        row_done = kx == qx
        last = jnp.logical_and(row_done, qx == nq - 1)
        return (jnp.where(jnp.logical_and(row_done, jnp.logical_not(last)), qx + 1, qx),
                jnp.where(last, kx, jnp.where(row_done, 0, kx + 1)))

    def body(_, pairs):
        pa, pb, pc = pairs
        step(0, (pa, pb, pc))
        pa2 = advance(pa)
        step(1, (pa2, pa, pb))
        return (advance(pa2), pa2, pa)

    origin = (jnp.int32(0), jnp.int32(0))
    lax.fori_loop(0, n_steps // 2, body, (origin, origin, origin))


def _alibi_tiles(tq):
    slopes = 2.0 ** (-8.0 * np.arange(1, DA_HEADS + 1) / DA_HEADS)
    kpos = np.arange(tq)[:, None]
    qpos = np.arange(tq)[None, :]
    low = slopes[:, None, None] * LOG2E * (kpos - qpos)[None]
    diag = np.where((kpos // CHUNK <= qpos // CHUNK)[None],
                    -slopes[:, None, None] * LOG2E * np.abs(qpos - kpos)[None], MASK_VALUE)
    dec = slopes * LOG2E * tq
    return jnp.asarray(dec, F32), jnp.asarray(np.stack([low, diag], axis=1), F32)


def _diff_attention(proj, lamv, subln_g, *, batch, seq, tq, lam_init):
    t = proj.shape[0]
    nq = seq // tq
    dec, bias = _alibi_tiles(tq)
    grid_spec = pltpu.PrefetchScalarGridSpec(
        num_scalar_prefetch=1,
        grid=(batch, DA_HEADS),
        in_specs=[
            pl.BlockSpec((seq, LANES), lambda b, h, dec: (b, h)),
            pl.BlockSpec((seq, LANES), lambda b, h, dec: (b, DA_HEADS + h)),
            pl.BlockSpec((seq, LANES), lambda b, h, dec: (b, 2 * DA_HEADS + h)),
            pl.BlockSpec((1, 2, tq, tq), lambda b, h, dec: (h, 0, 0, 0)),
            pl.BlockSpec((4, DA_HEAD_DIM), lambda b, h, dec: (0, 0)),
            pl.BlockSpec((LANES, 1), lambda b, h, dec: (0, 0)),
        ],
        out_specs=pl.BlockSpec((seq, LANES), lambda b, h, dec: (b, h)),
        scratch_shapes=[
            pltpu.VMEM((nq, LANES, tq), BF16),
            pltpu.VMEM((2, tq, tq), F32),
            pltpu.VMEM((2, tq, tq), F32),
            pltpu.VMEM((2, tq, tq), BF16),
            pltpu.VMEM((2, tq, tq), BF16),
            pltpu.VMEM((2, 8, tq), F32),
            pltpu.VMEM((2, 8, tq), F32),
            pltpu.VMEM((2, LANES, tq), F32),
        ],
    )
    return pl.pallas_call(
        functools.partial(_attn_kernel, tq=tq, lam_init=lam_init),
        out_shape=jax.ShapeDtypeStruct((t, DA_WIDTH), BF16),
        grid_spec=grid_spec,
        compiler_params=_params(("parallel", "parallel")),
        name="diff_attn",
    )(dec, proj, proj, proj, bias, lamv, subln_g.reshape(LANES, 1))


def _hgrn_kernel(q_ref, fz_ref, v_ref, gate_ref, lbl_ref, g_ref, o_ref, st_ref, *, layer, n_chunks):
    @pl.when(pl.program_id(2) == 0)
    def _():
        st_ref[...] = jnp.zeros_like(st_ref)

    lbl = lbl_ref[...]
    e = jnp.exp(lbl - jnp.max(lbl, axis=0, keepdims=True))
    pr = e / jnp.sum(e, axis=0, keepdims=True)
    lb = jnp.sum(pr[0:layer + 1], axis=0, keepdims=True) - pr[0:1]

    rr = lax.broadcasted_iota(jnp.int32, (CHUNK, CHUNK), 0)
    cc = lax.broadcasted_iota(jnp.int32, (CHUNK, CHUNK), 1)
    tril = (cc <= rr).astype(F32)
    sub_row = lax.broadcasted_iota(jnp.int32, (SUB_BLOCK, LANES), 0)
    n_sub = CHUNK // SUB_BLOCK
    gain = g_ref[...]

    z_all = fz_ref[...]
    f_all = lb + (1.0 - lb) * (1.0 / (1.0 + jnp.exp(-z_all)))
    k_all = (1.0 - lb) * (1.0 / (1.0 + jnp.exp(z_all)))
    logf = jnp.log(f_all)
    lf_hi = logf.astype(BF16)
    lf_r1 = logf - lf_hi.astype(F32)
    lf_mid = lf_r1.astype(BF16)
    lf_lo = (lf_r1 - lf_mid.astype(F32)).astype(BF16)
    lf3 = jnp.concatenate([lf_hi, lf_mid, lf_lo], axis=1)
    r2 = lax.broadcasted_iota(jnp.int32, (2 * CHUNK, CHUNK), 0)
    c2 = lax.broadcasted_iota(jnp.int32, (2 * CHUNK, CHUNK), 1)
    in_chunk = jnp.logical_and(r2 < CHUNK, c2 <= r2)
    in_sub = jnp.logical_and(jnp.logical_and(r2 >= CHUNK, c2 <= r2 - CHUNK),
                             c2 // SUB_BLOCK == (r2 - CHUNK) // SUB_BLOCK)
    cum_w = jnp.where(jnp.logical_or(in_chunk, in_sub), 1.0, 0.0).astype(BF16)
    cums = []
    for c in range(n_chunks):
        e3 = _dot(cum_w, lf3[c * CHUNK:(c + 1) * CHUNK])
        cums.append(e3[:, 0:LANES] + e3[:, LANES:2 * LANES] + e3[:, 2 * LANES:3 * LANES])
    min_rel = cums[0][CHUNK:]
    for c in range(1, n_chunks):
        min_rel = jnp.minimum(min_rel, cums[c][CHUNK:])
    safe = jnp.min(min_rel) >= HG_SAFE_LOG_DECAY

    blk_r, blk_c = rr // SUB_BLOCK, cc // SUB_BLOCK
    mask_sub = jnp.logical_and(blk_r == blk_c, cc <= rr)
    mask_16 = jnp.logical_and(blk_r % 2 == 1, blk_c == blk_r - 1)
    zeros16 = jnp.zeros((SUB_BLOCK, LANES), F32)
    zeros32 = jnp.zeros((2 * SUB_BLOCK, LANES), F32)

    @pl.when(safe)
    def _():
        st = st_ref[...]
        for c in range(n_chunks):
            rows = slice(c * CHUNK, (c + 1) * CHUNK)
            q = q_ref[rows, :].astype(F32)
            vb = v_ref[rows, :]
            k = k_all[rows]
            b, rel = cums[c][:CHUNK], cums[c][CHUNK:]
            b15, b31, b47, b63 = b[15:16], b[31:32], b[47:48], b[63:64]
            q32 = jnp.concatenate([zeros32, q[32:64] * jnp.exp(b[32:64] - b31)], axis=0)
            k32 = jnp.concatenate([k[0:32] * jnp.exp(b31 - b[0:32]), zeros32], axis=0)
            q16 = jnp.concatenate([zeros16, q[16:32] * jnp.exp(b[16:32] - b15),
                                   zeros16, q[48:64] * jnp.exp(b[48:64] - b47)], axis=0)
            k16 = jnp.concatenate([k[0:16] * jnp.exp(b15 - b[0:16]), zeros16,
                                   k[32:48] * jnp.exp(b47 - b[32:48]), zeros16], axis=0)
            attn = (_dot_nt(q32.astype(BF16), k32.astype(BF16))
                    + jnp.where(mask_16, _dot_nt(q16.astype(BF16), k16.astype(BF16)), 0.0)
                    + jnp.where(mask_sub, _dot_nt((q * jnp.exp(rel)).astype(BF16),
                                                  (k * jnp.exp(-rel)).astype(BF16)), 0.0))
            o = _dot(attn.astype(BF16), vb) + _dot_nt((q * jnp.exp(b)).astype(BF16), st.astype(BF16))
            upd = lax.dot_general(vb, (k * jnp.exp(b63 - b)).astype(BF16), (((0,), (0,)), ((), ())),
                                  preferred_element_type=F32)
            st = st * jnp.exp(b63) + upd
            gt = gate_ref[rows, :].astype(F32)
            o_ref[rows, :] = (_rms(o, gain) * (gt * (1.0 / (1.0 + jnp.exp(-gt))))).astype(BF16)
        st_ref[...] = st

    def chunk(c, _):
        rows = pl.ds(pl.multiple_of(c * CHUNK, CHUNK), CHUNK)
        z = fz_ref[rows, :]
        q = q_ref[rows, :].astype(F32)
        v = v_ref[rows, :].astype(F32)
        f = lb + (1.0 - lb) * (1.0 / (1.0 + jnp.exp(-z)))
        k = (1.0 - lb) * (1.0 / (1.0 + jnp.exp(z)))
        b = jnp.dot(tril, jnp.log(f), preferred_element_type=F32, precision=lax.Precision.HIGHEST)
        vb = v.astype(BF16)
        st = st_ref[...]
        o_inter = _dot_nt((q * jnp.exp(b)).astype(BF16), st.astype(BF16))

        outs = []
        for i in range(n_sub):
            r0 = i * SUB_BLOCK
            qi = q[r0:r0 + SUB_BLOCK]
            ki = k[r0:r0 + SUB_BLOCK]
            bi = b[r0:r0 + SUB_BLOCK]
            vi = v[r0:r0 + SUB_BLOCK]
            oi = jnp.zeros((SUB_BLOCK, LANES), F32)
            for s in range(SUB_BLOCK):
                dcy = jnp.where(sub_row >= s, jnp.exp(bi - bi[s:s + 1]), 0.0)
                a_s = jnp.sum(dcy * qi * ki[s:s + 1], axis=-1, keepdims=True)
                oi = oi + a_s * vi[s:s + 1]
            if i > 0:
                ref_b = b[r0 - 1:r0]
                qd = (qi * jnp.exp(bi - ref_b)).astype(BF16)
                kd = (k[0:r0] * jnp.exp(ref_b - b[0:r0])).astype(BF16)
                oi = oi + _dot(_dot_nt(qd, kd).astype(BF16), vb[0:r0])
            outs.append(oi)
        o = jnp.concatenate(outs, axis=0) + o_inter

        b_last = b[CHUNK - 1:CHUNK]
        kd = (k * jnp.exp(b_last - b)).astype(BF16)
        upd = lax.dot_general(vb, kd, (((0,), (0,)), ((), ())), preferred_element_type=F32)
        st_ref[...] = st * jnp.exp(b_last) + upd

        gt = gate_ref[rows, :].astype(F32)
        o_ref[rows, :] = (_rms(o, gain) * (gt * (1.0 / (1.0 + jnp.exp(-gt))))).astype(BF16)
        return 0

    @pl.when(jnp.logical_not(safe))
    def _():
        lax.fori_loop(0, n_chunks, chunk, 0)


def _hgrn(proj, fz32, lb_logits, out_norm_g, *, batch, seq, ts, layer):
    t = proj.shape[0]
    ns = seq // ts
    n_b = lb_logits.shape[0]
    return pl.pallas_call(
        functools.partial(_hgrn_kernel, layer=layer, n_chunks=ts // CHUNK),
        out_shape=jax.ShapeDtypeStruct((t, HG_VAL_WIDTH), BF16),
        grid=(batch, HG_HEADS, ns),
        in_specs=[
            pl.BlockSpec((ts, LANES), lambda b, h, i: (b * ns + i, h)),
            pl.BlockSpec((ts, LANES), lambda b, h, i: (b * ns + i, h)),
            pl.BlockSpec((ts, LANES), lambda b, h, i: (b * ns + i, 2 * HG_HEADS + h)),
            pl.BlockSpec((ts, LANES), lambda b, h, i: (b * ns + i, 3 * HG_HEADS + h)),
            pl.BlockSpec((n_b, LANES), lambda b, h, i: (0, h)),
            pl.BlockSpec((1, LANES), lambda b, h, i: (0, 0)),
        ],
        out_specs=pl.BlockSpec((ts, LANES), lambda b, h, i: (b * ns + i, h)),
        scratch_shapes=[pltpu.VMEM((HG_VAL_DIM, HG_KEY_DIM), F32)],
        compiler_params=_params(("parallel", "parallel", "arbitrary")),
        name="hgrn2",
    )(proj, fz32, proj, proj, lb_logits, out_norm_g.reshape(1, LANES))


def _outproj_kernel(o_ref, qm_ref, kv_ref, x_ref, w_ref, out_ref):
    qm = qm_ref[...]
    kmem = kv_ref[0, :, 0:MEM_WIDTH]
    vmem = kv_ref[0, :, MEM_WIDTH:2 * MEM_WIDTH]
    head = lax.broadcasted_iota(jnp.int32, qm.shape, 1) // MEM_HEAD_DIM
    om = jnp.zeros(qm.shape, F32)
    for hh in range(MEM_HEADS):
        qh = jnp.where(head == hh, qm, jnp.zeros_like(qm))
        s = _dot_nt(qh, kmem) * (MEM_HEAD_DIM ** -0.5)
        p = jnp.exp(s - jnp.max(s, axis=-1, keepdims=True))
        p = p / jnp.sum(p, axis=-1, keepdims=True)
        om = jnp.where(head == hh, _dot(p.astype(BF16), vmem), om)
    wo = o_ref.shape[1]
    out_ref[...] = (x_ref[...] + _dot(o_ref[...], w_ref[0:wo, :])
                    + _dot(om.astype(BF16), w_ref[wo:wo + MEM_WIDTH, :]))


def _outproj(o, proj, qm_block, memkv, layer, x2d, w_out_bf16, *, tm, seq, mem_len):
    t, d = x2d.shape
    wo = o.shape[1]
    tiles_per_batch = seq // tm
    return pl.pallas_call(
        _outproj_kernel,
        out_shape=jax.ShapeDtypeStruct((t, d), F32),
        grid=(t // tm,),
        in_specs=[
            pl.BlockSpec((tm, wo), lambda i: (i, 0)),
            pl.BlockSpec((tm, MEM_WIDTH), lambda i: (i, qm_block)),
            pl.BlockSpec((1, mem_len, 2 * MEM_WIDTH), lambda i: (layer, i // tiles_per_batch, 0)),
            pl.BlockSpec((tm, d), lambda i: (i, 0)),
            pl.BlockSpec((wo + MEM_WIDTH, d), lambda i: (0, 0)),
        ],
        out_specs=pl.BlockSpec((tm, d), lambda i: (i, 0)),
        compiler_params=_params(("parallel",)),
        name="outproj",
    )(o, proj, memkv, x2d, w_out_bf16)


def _ffn_kernel(te_ref, nt_ref, x_ref, g_ref, wg_ref, wu_ref, wd_ref, out_ref, xn_ref, acc_ref, *, dense):
    i = pl.program_id(0)
    j = pl.program_id(1)

    @pl.when(i < nt_ref[0])
    def _():
        @pl.when(j == 0)
        def _():
            x = x_ref[...]
            xn_ref[...] = (_rms(x, g_ref[...]) if dense else x).astype(BF16)

        xn = xn_ref[...]
        gt = _dot(xn, wg_ref[0])
        up = _dot(xn, wu_ref[0])
        act = (gt * (1.0 / (1.0 + jnp.exp(-gt))) * up).astype(BF16)
        part = _dot(act, wd_ref[0])

        @pl.when(j == 0)
        def _():
            acc_ref[...] = part

        @pl.when(j > 0)
        def _():
            acc_ref[...] += part

        @pl.when(j == pl.num_programs(1) - 1)
        def _():
            out_ref[...] = acc_ref[...] + x_ref[...] if dense else acc_ref[...]

    @pl.when(jnp.logical_and(i >= nt_ref[0], j == pl.num_programs(1) - 1))
    def _():
        out_ref[...] = jnp.zeros_like(out_ref)


def _ffn(tile_expert, n_tiles, x2d, g, w_gate_up_bf16, w_down_bf16, *, tm, tf, dense):
    r, d = x2d.shape
    f = w_down_bf16.shape[1]
    nf = f // tf
    grid_spec = pltpu.PrefetchScalarGridSpec(
        num_scalar_prefetch=2,
        grid=(r // tm, nf),
        in_specs=[
            pl.BlockSpec((tm, d), lambda i, j, te, nt: (i, 0)),
            pl.BlockSpec((1, d), lambda i, j, te, nt: (0, 0)),
            pl.BlockSpec((1, d, tf), lambda i, j, te, nt: (te[i], 0, j)),
            pl.BlockSpec((1, d, tf), lambda i, j, te, nt: (te[i], 0, nf + j)),
            pl.BlockSpec((1, tf, d), lambda i, j, te, nt: (te[i], j, 0)),
        ],
        out_specs=pl.BlockSpec((tm, d), lambda i, j, te, nt: (i, 0)),
        scratch_shapes=[pltpu.VMEM((tm, d), BF16), pltpu.VMEM((tm, d), F32)],
    )
    return pl.pallas_call(
        functools.partial(_ffn_kernel, dense=dense),
        out_shape=jax.ShapeDtypeStruct((r, d), F32),
        grid_spec=grid_spec,
        compiler_params=_params(("parallel", "arbitrary")),
        name="swiglu_dense" if dense else "swiglu_grouped",
    )(tile_expert, n_tiles, x2d, g.reshape(1, d), w_gate_up_bf16, w_gate_up_bf16, w_down_bf16)


META_E1, META_E2, META_W1, META_W2, META_R1, META_R2 = range(6)


def _router_kernel(x_ref, g_ref, wr_ref, meta_ref, cnt_ref, run_ref):
    @pl.when(pl.program_id(0) == 0)
    def _():
        run_ref[...] = jnp.zeros_like(run_ref)

    tm = x_ref.shape[0]
    xn = _rms(x_ref[...], g_ref[...])
    logits = jnp.dot(xn, wr_ref[...], preferred_element_type=F32, precision=lax.Precision.HIGHEST)
    lane = lax.broadcasted_iota(jnp.int32, logits.shape, 1)
    logits = jnp.where(lane < N_EXPERTS, logits, -jnp.inf)

    def top(vals):
        best = jnp.max(vals, axis=-1, keepdims=True)
        idx = jnp.min(jnp.where(vals == best, lane, LANES), axis=-1, keepdims=True)
        return best, idx

    v1, e1 = top(logits)
    v2, e2 = top(jnp.where(lane == e1, -jnp.inf, logits))
    ex = jnp.exp(v2 - v1)
    w1 = 1.0 / (1.0 + ex)
    w2 = ex / (1.0 + ex)

    oh1 = (lane == e1).astype(F32)
    oh2 = (lane == e2).astype(F32)
    rr = lax.broadcasted_iota(jnp.int32, (tm, tm), 0)
    cc = lax.broadcasted_iota(jnp.int32, (tm, tm), 1)
    strict = (cc < rr).astype(BF16)
    before = _dot(strict, (oh1 + oh2).astype(BF16)) + run_ref[0:1, :]
    r1 = jnp.sum(before * oh1, axis=-1, keepdims=True)
    r2 = jnp.sum(before * oh2, axis=-1, keepdims=True)
    run_ref[0:1, :] = run_ref[0:1, :] + jnp.sum(oh1 + oh2, axis=0, keepdims=True)

    meta = jnp.zeros(logits.shape, F32)
    for slot, val in ((META_E1, e1.astype(F32)), (META_E2, e2.astype(F32)), (META_W1, w1), (META_W2, w2),
                      (META_R1, r1), (META_R2, r2)):
        meta = jnp.where(lane == slot, val, meta)
    meta_ref[...] = meta
    cnt_ref[...] = run_ref[...]


def _router(x2d, g, w_router_padded, *, tm):
    t, d = x2d.shape
    return pl.pallas_call(
        _router_kernel,
        out_shape=[jax.ShapeDtypeStruct((t, LANES), F32), jax.ShapeDtypeStruct((8, LANES), F32)],
        grid=(t // tm,),
        in_specs=[
            pl.BlockSpec((tm, d), lambda i: (i, 0)),
            pl.BlockSpec((1, d), lambda i: (0, 0)),
            pl.BlockSpec((d, LANES), lambda i: (0, 0)),
        ],
        out_specs=[pl.BlockSpec((tm, LANES), lambda i: (i, 0)), pl.BlockSpec((8, LANES), lambda i: (0, 0))],
        scratch_shapes=[pltpu.VMEM((8, LANES), F32)],
        compiler_params=_params(("arbitrary",)),
        name="moe_router",
    )(x2d, g.reshape(1, d), w_router_padded)


def _dispatch_kernel(slot_ref, x_ref, g_ref, zeros_ref, xs_ref, xn_ref, sem):
    del zeros_ref
    tm = x_ref.shape[0]
    xn_ref[...] = _rms(x_ref[...], g_ref[...])

    def row_copy(r, k):
        return pltpu.make_async_copy(xn_ref.at[pl.ds(r, 1)], xs_ref.at[pl.ds(slot_ref[0, 0, 2 * r + k], 1)], sem)

    def start(r, _):
        row_copy(r, 0).start()
        row_copy(r, 1).start()
        return 0

    def wait(r, _):
        row_copy(r, 0).wait()
        row_copy(r, 1).wait()
        return 0

    lax.fori_loop(0, tm, start, 0)
    lax.fori_loop(0, tm, wait, 0)


def _dispatch(slots3, x2d, g, n_slots, *, tm):
    t, d = x2d.shape
    return pl.pallas_call(
        _dispatch_kernel,
        out_shape=jax.ShapeDtypeStruct((n_slots, d), F32),
        grid=(t // tm,),
        in_specs=[
            pl.BlockSpec((1, 1, 2 * tm), lambda i: (i, 0, 0), memory_space=pltpu.SMEM),
            pl.BlockSpec((tm, d), lambda i: (i, 0)),
            pl.BlockSpec((1, d), lambda i: (0, 0)),
            pl.BlockSpec(memory_space=pl.ANY),
        ],
        out_specs=pl.BlockSpec(memory_space=pl.ANY),
        scratch_shapes=[pltpu.VMEM((tm, d), F32), pltpu.SemaphoreType.DMA],
        input_output_aliases={3: 0},
        compiler_params=_params(("arbitrary",)),
        name="moe_dispatch",
    )(slots3, x2d, g.reshape(1, d), jnp.zeros((n_slots, d), F32))


def _combine_kernel(slot_ref, x_ref, meta_ref, y_ref, g_ref, out_ref, ybuf_ref, sem, *, final_norm):
    tm = x_ref.shape[0]

    def row_copy(r, k):
        return pltpu.make_async_copy(y_ref.at[pl.ds(slot_ref[0, 0, 2 * r + k], 1)], ybuf_ref.at[k, pl.ds(r, 1)], sem)

    def start(r, _):
        row_copy(r, 0).start()
        row_copy(r, 1).start()
        return 0

    def wait(r, _):
        row_copy(r, 0).wait()
        row_copy(r, 1).wait()
        return 0

    lax.fori_loop(0, tm, start, 0)
    lax.fori_loop(0, tm, wait, 0)
    meta = meta_ref[...]
    w1 = meta[:, META_W1:META_W1 + 1]
    w2 = meta[:, META_W2:META_W2 + 1]
    out = x_ref[...] + (w1 * ybuf_ref[0] + w2 * ybuf_ref[1])
    out_ref[...] = _rms(out, g_ref[...]) if final_norm else out


def _combine(slots3, x2d, meta, y, g, *, tm, final_norm):
    t, d = x2d.shape
    return pl.pallas_call(
        functools.partial(_combine_kernel, final_norm=final_norm),
        out_shape=jax.ShapeDtypeStruct((t, d), F32),
        grid=(t // tm,),
        in_specs=[
            pl.BlockSpec((1, 1, 2 * tm), lambda i: (i, 0, 0), memory_space=pltpu.SMEM),
            pl.BlockSpec((tm, d), lambda i: (i, 0)),
            pl.BlockSpec((tm, LANES), lambda i: (i, 0)),
            pl.BlockSpec(memory_space=pl.ANY),
            pl.BlockSpec((1, d), lambda i: (0, 0)),
        ],
        out_specs=pl.BlockSpec((tm, d), lambda i: (i, 0)),
        scratch_shapes=[pltpu.VMEM((2, tm, d), F32), pltpu.SemaphoreType.DMA],
        compiler_params=_params(("arbitrary",)),
        name="moe_combine",
    )(slots3, x2d, meta, y, g.reshape(1, d))


def _moe(x2d, norm_g, w_router, w_gate_up, w_down, final_g, *, tm_route, tm_row, tm_ffn, tf, final_norm):
    t, d = x2d.shape
    wr = jnp.zeros((d, LANES), F32).at[:, :N_EXPERTS].set(w_router)
    meta, counts = _router(x2d, norm_g, wr, tm=tm_route)

    cnt = counts[0, :N_EXPERTS].astype(jnp.int32)
    padded = (cnt + tm_ffn - 1) // tm_ffn * tm_ffn
    ends = jnp.cumsum(padded)
    starts = ends - padded
    e12 = meta[:, META_E1:META_E2 + 1].astype(jnp.int32)
    r12 = meta[:, META_R1:META_R2 + 1].astype(jnp.int32)
    slots = starts[e12] + r12
    slots3 = slots.reshape(t // tm_row, 1, 2 * tm_row)
    n_tiles_max = (2 * t) // tm_ffn + N_EXPERTS
    tile_start = jnp.arange(n_tiles_max, dtype=jnp.int32) * tm_ffn
    tile_expert = jnp.minimum(
        jnp.sum((ends[None, :] <= tile_start[:, None]).astype(jnp.int32), axis=1), N_EXPERTS - 1)
    n_tiles = (ends[N_EXPERTS - 1] // tm_ffn).astype(jnp.int32).reshape(1)

    xs = _dispatch(slots3, x2d, norm_g, n_tiles_max * tm_ffn, tm=tm_row)
    ys = _ffn(tile_expert, n_tiles, xs, norm_g, w_gate_up.astype(BF16), w_down.astype(BF16),
              tm=tm_ffn, tf=tf, dense=False)
    return _combine(slots3, x2d, meta, ys, final_g, tm=tm_row, final_norm=final_norm)


def _tiles(batch, seq):
    full = seq >= 4096
    return dict(
        tm_proj=512 if full else 128,
        tq=256 if full else 128,
        ts=512 if full else 128,
        tm_out=512 if full else 128,
        tm_dense=512 if full else 128,
        tm_route=512 if full else 128,
        tm_row=256 if full else 128,
        tm_ffn=1024 if full else 128,
    )


def kernel(x, mem, a_norm_mix, a_w_in, a_lam_q1, a_lam_k1, a_lam_q2, a_lam_k2, a_subln, a_mem_norm, a_w_mem_kv, a_w_out, b_norm_mix, b_w_in, b_lb_logits, b_out_norm, b_mem_norm, b_w_mem_kv, b_w_out, dense_norm, dense_w_gate_up, dense_w_down, moe_norm, moe_router, moe_w_gate_up, moe_w_down, final_norm):
    batch, seq, d = x.shape
    mem_len = mem.shape[1]
    t = batch * seq
    ts = _tiles(batch, seq)
    x2d = x.reshape(t, d)
    mem2d = mem.reshape(batch * mem_len, d)
    kv_a = _memkv(mem2d, a_mem_norm, a_w_mem_kv, mem_len)
    kv_b = _memkv(mem2d, b_mem_norm, b_w_mem_kv, mem_len)
    one_tile = jnp.zeros((t // ts["tm_dense"],), jnp.int32)
    all_tiles = jnp.full((1,), t // ts["tm_dense"], jnp.int32)
    d_ff = dense_w_down.shape[1]
    tf_dense = d_ff // 2 if (d_ff // 2) % LANES == 0 else d_ff
    tf_moe = 512 if moe_w_down.shape[2] % 512 == 0 else moe_w_down.shape[2]

    for i in range(DEPTH):
        j = i // N_MIXERS
        if i % N_MIXERS == 0:
            lam_init = 0.8 - 0.6 * float(np.exp(-0.3 * i))
            proj = _normproj(x2d, a_norm_mix[j], a_w_in[j].astype(BF16), tm=ts["tm_proj"],
                             scaled_cols=DA_WIDTH, col_scale=DA_HEAD_DIM ** -0.5 * LOG2E)
            lamv = jnp.stack([a_lam_q1[j], a_lam_k1[j], a_lam_q2[j], a_lam_k2[j]])
            o = _diff_attention(proj, lamv, a_subln[j], batch=batch, seq=seq, tq=ts["tq"], lam_init=lam_init)
            x2d = _outproj(o, proj, 3 * DA_WIDTH // MEM_WIDTH, kv_a, j, x2d, a_w_out[j].astype(BF16),
                           tm=ts["tm_out"], seq=seq, mem_len=mem_len)
        else:
            proj, fz32 = _normproj(x2d, b_norm_mix[j], b_w_in[j].astype(BF16), tm=ts["tm_proj"],
                                   f32_cols=(HG_KEY_WIDTH, 2 * HG_KEY_WIDTH))
            o = _hgrn(proj, fz32, b_lb_logits, b_out_norm[j], batch=batch, seq=seq, ts=ts["ts"], layer=j)
            x2d = _outproj(o, proj, (2 * HG_KEY_WIDTH + 2 * HG_VAL_WIDTH) // MEM_WIDTH, kv_b, j, x2d,
                           b_w_out[j].astype(BF16), tm=ts["tm_out"], seq=seq, mem_len=mem_len)
        if i % 2 == 0:
            x2d = _ffn(one_tile, all_tiles, x2d, dense_norm[j], dense_w_gate_up[j].astype(BF16)[None],
                       dense_w_down[j].astype(BF16)[None], tm=ts["tm_dense"], tf=tf_dense, dense=True)
        else:
            x2d = _moe(x2d, moe_norm[j], moe_router[j], moe_w_gate_up[j], moe_w_down[j], final_norm,
                       tm_route=ts["tm_route"], tm_row=ts["tm_row"], tm_ffn=ts["tm_ffn"], tf=tf_moe,
                       final_norm=(i == DEPTH - 1))
    return x2d.reshape(batch, seq, d)
```

```python
import functools

import numpy as np
import jax
import jax.numpy as jnp
from jax import lax
from jax.experimental import pallas as pl
from jax.experimental.pallas import tpu as pltpu

F32 = jnp.float32
BF16 = jnp.bfloat16

DEPTH = 4
N_MIXERS = 2
CHUNK = 64
DA_HEADS = 6
DA_HEAD_DIM = 64
DA_WIDTH = DA_HEADS * 2 * DA_HEAD_DIM
HG_HEADS = 6
HG_KEY_DIM = 128
HG_VAL_DIM = 128
HG_KEY_WIDTH = HG_HEADS * HG_KEY_DIM
HG_VAL_WIDTH = HG_HEADS * HG_VAL_DIM
MEM_HEADS = 4
MEM_HEAD_DIM = 64
MEM_WIDTH = MEM_HEADS * MEM_HEAD_DIM
N_EXPERTS = 8
EPS = 1e-6
MASK_VALUE = -1e30
LOG2E = float(np.log2(np.e))

LANES = 128
SUB_BLOCK = 16
HG_SAFE_LOG_DECAY = -64.0
VMEM_LIMIT = 56 * 1024 * 1024


def _params(sem):
    return pltpu.CompilerParams(dimension_semantics=sem, vmem_limit_bytes=VMEM_LIMIT)


def _rms(x, g):
    return x * lax.rsqrt(jnp.mean(x * x, axis=-1, keepdims=True) + EPS) * g


def _dot(a, b):
    return jnp.dot(a, b, preferred_element_type=F32)


def _dot_nt(a, b):
    return lax.dot_general(a, b, (((1,), (1,)), ((), ())), preferred_element_type=F32)


def _memkv_kernel(mem_ref, g_ref, w_ref, out_ref):
    y = _rms(mem_ref[...], g_ref[0]).astype(BF16)
    out_ref[0] = _dot(y, w_ref[0].astype(BF16)).astype(BF16)


def _memkv(mem2d, g, w, mem_len):
    n_layers, d, n = w.shape
    rows = mem2d.shape[0]
    return pl.pallas_call(
        _memkv_kernel,
        out_shape=jax.ShapeDtypeStruct((n_layers, rows, n), BF16),
        grid=(n_layers, rows // mem_len),
        in_specs=[
            pl.BlockSpec((mem_len, d), lambda l, b: (b, 0)),
            pl.BlockSpec((1, 1, d), lambda l, b: (l, 0, 0)),
            pl.BlockSpec((1, d, n), lambda l, b: (l, 0, 0)),
        ],
        out_specs=pl.BlockSpec((1, mem_len, n), lambda l, b: (l, b, 0)),
        compiler_params=_params(("arbitrary", "arbitrary")),
        name="memkv",
    )(mem2d, g.reshape(n_layers, 1, d), w)


NP_CHUNK = 256


def _normproj_kernel(x_ref, g_ref, w_ref, out_ref, *f32_refs, scaled_cols, col_scale, f32_cols):
    xn = _rms(x_ref[...], g_ref[...]).astype(BF16)
    n = w_ref.shape[1]
    for c0 in range(0, n, NP_CHUNK):
        y = _dot(xn, w_ref[:, c0:c0 + NP_CHUNK])
        if c0 < scaled_cols:
            y = y * col_scale
        out_ref[:, c0:c0 + NP_CHUNK] = y.astype(BF16)
        if f32_cols is not None and f32_cols[0] <= c0 < f32_cols[1]:
            f32_refs[0][:, c0 - f32_cols[0]:c0 - f32_cols[0] + NP_CHUNK] = y


def _normproj(x2d, g, w_bf16, *, tm, scaled_cols=0, col_scale=1.0, f32_cols=None):
    t, d = x2d.shape
    n = w_bf16.shape[1]
    assert n % NP_CHUNK == 0 and scaled_cols % NP_CHUNK == 0
    out_shape = [jax.ShapeDtypeStruct((t, n), BF16)]
    out_specs = [pl.BlockSpec((tm, n), lambda i: (i, 0))]
    if f32_cols is not None:
        assert f32_cols[0] % NP_CHUNK == 0 and f32_cols[1] % NP_CHUNK == 0
        wf = f32_cols[1] - f32_cols[0]
        out_shape.append(jax.ShapeDtypeStruct((t, wf), F32))
        out_specs.append(pl.BlockSpec((tm, wf), lambda i: (i, 0)))
    outs = pl.pallas_call(
        functools.partial(_normproj_kernel, scaled_cols=scaled_cols, col_scale=col_scale, f32_cols=f32_cols),
        out_shape=out_shape,
        grid=(t // tm,),
        in_specs=[
            pl.BlockSpec((tm, d), lambda i: (i, 0)),
            pl.BlockSpec((1, d), lambda i: (0, 0)),
            pl.BlockSpec((d, n), lambda i: (0, 0)),
        ],
        out_specs=out_specs,
        compiler_params=_params(("parallel",)),
        name="normproj",
    )(x2d, g.reshape(1, d), w_bf16)
    return outs if f32_cols is not None else outs[0]


ATT_ROWS = 32


def _fold8(x, op):
    out = x[0:8]
    for r in range(8, x.shape[0], 8):
        out = op(out, x[r:r + 8])
    return out


def _attn_kernel(dec_ref, q_ref, k_ref, v_ref, bias_ref, lamv_ref, g_ref, o_ref,
                 vt_ref, s0_ref, s1_ref, mx0_ref, mx1_ref, p_ref, st_ref, acc_ref, *, tq, lam_init):
    h = pl.program_id(1)
    nq = q_ref.shape[0] // tq
    n_pairs = nq * (nq + 1) // 2
    assert n_pairs % 2 == 0
    s_refs, mx_refs = (s0_ref, s1_ref), (mx0_ref, mx1_ref)

    for j in range(nq):
        vt_ref[j] = v_ref[j * tq:(j + 1) * tq, :].astype(F32).T.astype(BF16)
    for ref in (s1_ref, mx1_ref, acc_ref, st_ref):
        ref[...] = jnp.zeros_like(ref)

    dec = dec_ref[h]
    lo_map = lax.broadcasted_iota(jnp.int32, (tq, LANES), 1) < DA_HEAD_DIM
    lv = lamv_ref[...]
    lam = (jnp.exp(jnp.sum(lv[0:1] * lv[1:2], axis=-1, keepdims=True))
           - jnp.exp(jnp.sum(lv[2:3] * lv[3:4], axis=-1, keepdims=True)) + lam_init)
    out_gain = g_ref[...] * (1.0 - lam_init)

    def step(slot, pair_a, pair_b):
        (qa, ka), (qb, kb) = pair_a, pair_b
        s_a, s_b = s_refs[slot], s_refs[1 - slot]

        q = q_ref[pl.ds(pl.multiple_of(qa * tq, tq), tq), :]
        kt = k_ref[pl.ds(pl.multiple_of(ka * tq, tq), tq), :]
        zero = jnp.zeros_like(q)
        bias = bias_ref[0, (ka == qa).astype(jnp.int32)]
        for mi, qz in enumerate((jnp.where(lo_map, q, zero), jnp.where(lo_map, zero, q))):
            s = _dot_nt(kt, qz) + bias
            s_a[mi] = s
            mx_refs[slot][mi] = _fold8(s, jnp.maximum)

        first = kb == 0
        vt = vt_ref[kb]
        for mi in range(2):
            m_old = jnp.where(first, MASK_VALUE, st_ref[mi, 0:1, :])
            l_old = jnp.where(first, 0.0, st_ref[mi, 1:2, :])
            mp = m_old - dec
            mn = jnp.maximum(mp, jnp.max(mx_refs[1 - slot][mi], axis=0, keepdims=True))
            alpha = jnp.exp2(mp - mn)
            l8 = None
            for c in range(0, tq, ATT_ROWS):
                p = jnp.exp2(s_b[mi, c:c + ATT_ROWS, :] - mn)
                p_ref[mi, c:c + ATT_ROWS, :] = p.astype(BF16)
                cs = _fold8(p, jnp.add)
                l8 = cs if l8 is None else l8 + cs
            st_ref[mi, 0:1, :] = mn
            st_ref[mi, 1:2, :] = alpha * l_old + jnp.sum(l8, axis=0, keepdims=True)
            kept = jnp.where(first, 0.0, alpha * acc_ref[mi])
            acc_ref[mi] = kept + _dot(vt, p_ref[mi])

        @pl.when(kb == qb)
        def _():
            ot = acc_ref[0] / st_ref[0, 1:2, :] - lam * (acc_ref[1] / st_ref[1, 1:2, :])
            ot = ot * lax.rsqrt(jnp.mean(ot * ot, axis=0, keepdims=True) + EPS) * out_gain
            o_ref[pl.ds(pl.multiple_of(qb * tq, tq), tq), :] = ot.T.astype(BF16)

    def advance(pair):
        qx, kx = pair
        row_done = kx == qx
        last = jnp.logical_and(row_done, qx == nq - 1)
        return (jnp.where(jnp.logical_and(row_done, jnp.logical_not(last)), qx + 1, qx),
                jnp.where(last, kx, jnp.where(row_done, 0, kx + 1)))

    def body(_, pairs):
        pa, pb = pairs
        step(0, pa, pb)
        pa2 = advance(pa)
        step(1, pa2, pa)
        return (advance(pa2), pa2)

    origin = (jnp.int32(0), jnp.int32(0))
    pa, pb = lax.fori_loop(0, n_pairs // 2, body, (origin, origin))
    step(0, pa, pb)


def _alibi_tiles(tq):
    slopes = 2.0 ** (-8.0 * np.arange(1, DA_HEADS + 1) / DA_HEADS)
    kpos = np.arange(tq)[:, None]
    qpos = np.arange(tq)[None, :]
    low = slopes[:, None, None] * LOG2E * (kpos - qpos)[None]
    diag = np.where((kpos // CHUNK <= qpos // CHUNK)[None],
                    -slopes[:, None, None] * LOG2E * np.abs(qpos - kpos)[None], MASK_VALUE)
    dec = slopes * LOG2E * tq
    return jnp.asarray(dec, F32), jnp.asarray(np.stack([low, diag], axis=1), F32)


def _diff_attention(proj, lamv, subln_g, *, batch, seq, tq, lam_init):
    t = proj.shape[0]
    nq = seq // tq
    dec, bias = _alibi_tiles(tq)
    grid_spec = pltpu.PrefetchScalarGridSpec(
        num_scalar_prefetch=1,
        grid=(batch, DA_HEADS),
        in_specs=[
            pl.BlockSpec((seq, LANES), lambda b, h, dec: (b, h)),
            pl.BlockSpec((seq, LANES), lambda b, h, dec: (b, DA_HEADS + h)),
            pl.BlockSpec((seq, LANES), lambda b, h, dec: (b, 2 * DA_HEADS + h)),
            pl.BlockSpec((1, 2, tq, tq), lambda b, h, dec: (h, 0, 0, 0)),
            pl.BlockSpec((4, DA_HEAD_DIM), lambda b, h, dec: (0, 0)),
            pl.BlockSpec((LANES, 1), lambda b, h, dec: (0, 0)),
        ],
        out_specs=pl.BlockSpec((seq, LANES), lambda b, h, dec: (b, h)),
        scratch_shapes=[
            pltpu.VMEM((nq, LANES, tq), BF16),
            pltpu.VMEM((2, tq, tq), F32),
            pltpu.VMEM((2, tq, tq), F32),
            pltpu.VMEM((2, 8, tq), F32),
            pltpu.VMEM((2, 8, tq), F32),
            pltpu.VMEM((2, tq, tq), BF16),
            pltpu.VMEM((2, 8, tq), F32),
            pltpu.VMEM((2, LANES, tq), F32),
        ],
    )
    return pl.pallas_call(
        functools.partial(_attn_kernel, tq=tq, lam_init=lam_init),
        out_shape=jax.ShapeDtypeStruct((t, DA_WIDTH), BF16),
        grid_spec=grid_spec,
        compiler_params=_params(("parallel", "parallel")),
        name="diff_attn",
    )(dec, proj, proj, proj, bias, lamv, subln_g.reshape(LANES, 1))


def _hgrn_kernel(q_ref, fz_ref, v_ref, gate_ref, lbl_ref, g_ref, o_ref, st_ref, *, layer, n_chunks):
    @pl.when(pl.program_id(2) == 0)
    def _():
        st_ref[...] = jnp.zeros_like(st_ref)

    lbl = lbl_ref[...]
    e = jnp.exp(lbl - jnp.max(lbl, axis=0, keepdims=True))
    pr = e / jnp.sum(e, axis=0, keepdims=True)
    lb = jnp.sum(pr[0:layer + 1], axis=0, keepdims=True) - pr[0:1]

    rr = lax.broadcasted_iota(jnp.int32, (CHUNK, CHUNK), 0)
    cc = lax.broadcasted_iota(jnp.int32, (CHUNK, CHUNK), 1)
    tril = (cc <= rr).astype(F32)
    sub_row = lax.broadcasted_iota(jnp.int32, (SUB_BLOCK, LANES), 0)
    n_sub = CHUNK // SUB_BLOCK
    gain = g_ref[...]

    z_all = fz_ref[...]
    f_all = lb + (1.0 - lb) * (1.0 / (1.0 + jnp.exp(-z_all)))
    k_all = (1.0 - lb) * (1.0 / (1.0 + jnp.exp(z_all)))
    logf = jnp.log(f_all)
    lf_hi = logf.astype(BF16)
    lf_r1 = logf - lf_hi.astype(F32)
    lf_mid = lf_r1.astype(BF16)
    lf_lo = (lf_r1 - lf_mid.astype(F32)).astype(BF16)
    lf3 = jnp.concatenate([lf_hi, lf_mid, lf_lo], axis=1)
    r2 = lax.broadcasted_iota(jnp.int32, (2 * CHUNK, CHUNK), 0)
    c2 = lax.broadcasted_iota(jnp.int32, (2 * CHUNK, CHUNK), 1)
    in_chunk = jnp.logical_and(r2 < CHUNK, c2 <= r2)
    in_sub = jnp.logical_and(jnp.logical_and(r2 >= CHUNK, c2 <= r2 - CHUNK),
                             c2 // SUB_BLOCK == (r2 - CHUNK) // SUB_BLOCK)
    cum_w = jnp.where(jnp.logical_or(in_chunk, in_sub), 1.0, 0.0).astype(BF16)
    cums = []
    for c in range(n_chunks):
        e3 = _dot(cum_w, lf3[c * CHUNK:(c + 1) * CHUNK])
        cums.append(e3[:, 0:LANES] + e3[:, LANES:2 * LANES] + e3[:, 2 * LANES:3 * LANES])
    min_rel = cums[0][CHUNK:]
    for c in range(1, n_chunks):
        min_rel = jnp.minimum(min_rel, cums[c][CHUNK:])
    safe = jnp.min(min_rel) >= HG_SAFE_LOG_DECAY

    blk_r, blk_c = rr // SUB_BLOCK, cc // SUB_BLOCK
    mask_sub = jnp.logical_and(blk_r == blk_c, cc <= rr)
    mask_16 = jnp.logical_and(blk_r % 2 == 1, blk_c == blk_r - 1)
    zeros16 = jnp.zeros((SUB_BLOCK, LANES), F32)
    zeros32 = jnp.zeros((2 * SUB_BLOCK, LANES), F32)

    @pl.when(safe)
    def _():
        st = st_ref[...]
        for c in range(n_chunks):
            rows = slice(c * CHUNK, (c + 1) * CHUNK)
            q = q_ref[rows, :].astype(F32)
            vb = v_ref[rows, :]
            k = k_all[rows]
            b, rel = cums[c][:CHUNK], cums[c][CHUNK:]
            b15, b31, b47, b63 = b[15:16], b[31:32], b[47:48], b[63:64]
            q32 = jnp.concatenate([zeros32, q[32:64] * jnp.exp(b[32:64] - b31)], axis=0)
            k32 = jnp.concatenate([k[0:32] * jnp.exp(b31 - b[0:32]), zeros32], axis=0)
            q16 = jnp.concatenate([zeros16, q[16:32] * jnp.exp(b[16:32] - b15),
                                   zeros16, q[48:64] * jnp.exp(b[48:64] - b47)], axis=0)
            k16 = jnp.concatenate([k[0:16] * jnp.exp(b15 - b[0:16]), zeros16,
                                   k[32:48] * jnp.exp(b47 - b[32:48]), zeros16], axis=0)
            attn = (_dot_nt(q32.astype(BF16), k32.astype(BF16))
                    + jnp.where(mask_16, _dot_nt(q16.astype(BF16), k16.astype(BF16)), 0.0)
                    + jnp.where(mask_sub, _dot_nt((q * jnp.exp(rel)).astype(BF16),
                                                  (k * jnp.exp(-rel)).astype(BF16)), 0.0))
            o = _dot(attn.astype(BF16), vb) + _dot_nt((q * jnp.exp(b)).astype(BF16), st.astype(BF16))
            upd = lax.dot_general(vb, (k * jnp.exp(b63 - b)).astype(BF16), (((0,), (0,)), ((), ())),
                                  preferred_element_type=F32)
            st = st * jnp.exp(b63) + upd
            gt = gate_ref[rows, :].astype(F32)
            o_ref[rows, :] = (_rms(o, gain) * (gt * (1.0 / (1.0 + jnp.exp(-gt))))).astype(BF16)
        st_ref[...] = st

    def chunk(c, _):
        rows = pl.ds(pl.multiple_of(c * CHUNK, CHUNK), CHUNK)
        z = fz_ref[rows, :]
        q = q_ref[rows, :].astype(F32)
        v = v_ref[rows, :].astype(F32)
        f = lb + (1.0 - lb) * (1.0 / (1.0 + jnp.exp(-z)))
        k = (1.0 - lb) * (1.0 / (1.0 + jnp.exp(z)))
        b = jnp.dot(tril, jnp.log(f), preferred_element_type=F32, precision=lax.Precision.HIGHEST)
        vb = v.astype(BF16)
        st = st_ref[...]
        o_inter = _dot_nt((q * jnp.exp(b)).astype(BF16), st.astype(BF16))

        outs = []
        for i in range(n_sub):
            r0 = i * SUB_BLOCK
            qi = q[r0:r0 + SUB_BLOCK]
            ki = k[r0:r0 + SUB_BLOCK]
            bi = b[r0:r0 + SUB_BLOCK]
            vi = v[r0:r0 + SUB_BLOCK]
            oi = jnp.zeros((SUB_BLOCK, LANES), F32)
            for s in range(SUB_BLOCK):
                dcy = jnp.where(sub_row >= s, jnp.exp(bi - bi[s:s + 1]), 0.0)
                a_s = jnp.sum(dcy * qi * ki[s:s + 1], axis=-1, keepdims=True)
                oi = oi + a_s * vi[s:s + 1]
            if i > 0:
                ref_b = b[r0 - 1:r0]
                qd = (qi * jnp.exp(bi - ref_b)).astype(BF16)
                kd = (k[0:r0] * jnp.exp(ref_b - b[0:r0])).astype(BF16)
                oi = oi + _dot(_dot_nt(qd, kd).astype(BF16), vb[0:r0])
            outs.append(oi)
        o = jnp.concatenate(outs, axis=0) + o_inter

        b_last = b[CHUNK - 1:CHUNK]
        kd = (k * jnp.exp(b_last - b)).astype(BF16)
        upd = lax.dot_general(vb, kd, (((0,), (0,)), ((), ())), preferred_element_type=F32)
        st_ref[...] = st * jnp.exp(b_last) + upd

        gt = gate_ref[rows, :].astype(F32)
        o_ref[rows, :] = (_rms(o, gain) * (gt * (1.0 / (1.0 + jnp.exp(-gt))))).astype(BF16)
        return 0

    @pl.when(jnp.logical_not(safe))
    def _():
        lax.fori_loop(0, n_chunks, chunk, 0)


def _hgrn(proj, fz32, lb_logits, out_norm_g, *, batch, seq, ts, layer):
    t = proj.shape[0]
    ns = seq // ts
    n_b = lb_logits.shape[0]
    return pl.pallas_call(
        functools.partial(_hgrn_kernel, layer=layer, n_chunks=ts // CHUNK),
        out_shape=jax.ShapeDtypeStruct((t, HG_VAL_WIDTH), BF16),
        grid=(batch, HG_HEADS, ns),
        in_specs=[
            pl.BlockSpec((ts, LANES), lambda b, h, i: (b * ns + i, h)),
            pl.BlockSpec((ts, LANES), lambda b, h, i: (b * ns + i, h)),
            pl.BlockSpec((ts, LANES), lambda b, h, i: (b * ns + i, 2 * HG_HEADS + h)),
            pl.BlockSpec((ts, LANES), lambda b, h, i: (b * ns + i, 3 * HG_HEADS + h)),
            pl.BlockSpec((n_b, LANES), lambda b, h, i: (0, h)),
            pl.BlockSpec((1, LANES), lambda b, h, i: (0, 0)),
        ],
        out_specs=pl.BlockSpec((ts, LANES), lambda b, h, i: (b * ns + i, h)),
        scratch_shapes=[pltpu.VMEM((HG_VAL_DIM, HG_KEY_DIM), F32)],
        compiler_params=_params(("parallel", "parallel", "arbitrary")),
        name="hgrn2",
    )(proj, fz32, proj, proj, lb_logits, out_norm_g.reshape(1, LANES))


def _outproj_kernel(o_ref, qm_ref, kv_ref, x_ref, w_ref, out_ref):
    qm = qm_ref[...]
    kmem = kv_ref[0, :, 0:MEM_WIDTH]
    vmem = kv_ref[0, :, MEM_WIDTH:2 * MEM_WIDTH]
    head = lax.broadcasted_iota(jnp.int32, qm.shape, 1) // MEM_HEAD_DIM
    om = jnp.zeros(qm.shape, F32)
    for hh in range(MEM_HEADS):
        qh = jnp.where(head == hh, qm, jnp.zeros_like(qm))
        s = _dot_nt(qh, kmem) * (MEM_HEAD_DIM ** -0.5)
        p = jnp.exp(s - jnp.max(s, axis=-1, keepdims=True))
        p = p / jnp.sum(p, axis=-1, keepdims=True)
        om = jnp.where(head == hh, _dot(p.astype(BF16), vmem), om)
    wo = o_ref.shape[1]
    out_ref[...] = (x_ref[...] + _dot(o_ref[...], w_ref[0:wo, :])
                    + _dot(om.astype(BF16), w_ref[wo:wo + MEM_WIDTH, :]))


def _outproj(o, proj, qm_block, memkv, layer, x2d, w_out_bf16, *, tm, seq, mem_len):
    t, d = x2d.shape
    wo = o.shape[1]
    tiles_per_batch = seq // tm
    return pl.pallas_call(
        _outproj_kernel,
        out_shape=jax.ShapeDtypeStruct((t, d), F32),
        grid=(t // tm,),
        in_specs=[
            pl.BlockSpec((tm, wo), lambda i: (i, 0)),
            pl.BlockSpec((tm, MEM_WIDTH), lambda i: (i, qm_block)),
            pl.BlockSpec((1, mem_len, 2 * MEM_WIDTH), lambda i: (layer, i // tiles_per_batch, 0)),
            pl.BlockSpec((tm, d), lambda i: (i, 0)),
            pl.BlockSpec((wo + MEM_WIDTH, d), lambda i: (0, 0)),
        ],
        out_specs=pl.BlockSpec((tm, d), lambda i: (i, 0)),
        compiler_params=_params(("parallel",)),
        name="outproj",
    )(o, proj, memkv, x2d, w_out_bf16)


def _ffn_kernel(te_ref, nt_ref, x_ref, g_ref, wg_ref, wu_ref, wd_ref, out_ref, xn_ref, acc_ref, *, dense):
    i = pl.program_id(0)
    j = pl.program_id(1)

    @pl.when(i < nt_ref[0])
    def _():
        @pl.when(j == 0)
        def _():
            x = x_ref[...]
            xn_ref[...] = (_rms(x, g_ref[...]) if dense else x).astype(BF16)

        xn = xn_ref[...]
        gt = _dot(xn, wg_ref[0])
        up = _dot(xn, wu_ref[0])
        act = (gt * (1.0 / (1.0 + jnp.exp(-gt))) * up).astype(BF16)
        part = _dot(act, wd_ref[0])

        @pl.when(j == 0)
        def _():
            acc_ref[...] = part

        @pl.when(j > 0)
        def _():
            acc_ref[...] += part

        @pl.when(j == pl.num_programs(1) - 1)
        def _():
            out_ref[...] = acc_ref[...] + x_ref[...] if dense else acc_ref[...]

    @pl.when(jnp.logical_and(i >= nt_ref[0], j == pl.num_programs(1) - 1))
    def _():
        out_ref[...] = jnp.zeros_like(out_ref)


def _ffn(tile_expert, n_tiles, x2d, g, w_gate_up_bf16, w_down_bf16, *, tm, tf, dense):
    r, d = x2d.shape
    f = w_down_bf16.shape[1]
    nf = f // tf
    grid_spec = pltpu.PrefetchScalarGridSpec(
        num_scalar_prefetch=2,
        grid=(r // tm, nf),
        in_specs=[
            pl.BlockSpec((tm, d), lambda i, j, te, nt: (i, 0)),
            pl.BlockSpec((1, d), lambda i, j, te, nt: (0, 0)),
            pl.BlockSpec((1, d, tf), lambda i, j, te, nt: (te[i], 0, j)),
            pl.BlockSpec((1, d, tf), lambda i, j, te, nt: (te[i], 0, nf + j)),
            pl.BlockSpec((1, tf, d), lambda i, j, te, nt: (te[i], j, 0)),
        ],
        out_specs=pl.BlockSpec((tm, d), lambda i, j, te, nt: (i, 0)),
        scratch_shapes=[pltpu.VMEM((tm, d), BF16), pltpu.VMEM((tm, d), F32)],
    )
    return pl.pallas_call(
        functools.partial(_ffn_kernel, dense=dense),
        out_shape=jax.ShapeDtypeStruct((r, d), F32),
        grid_spec=grid_spec,
        compiler_params=_params(("parallel", "arbitrary")),
        name="swiglu_dense" if dense else "swiglu_grouped",
    )(tile_expert, n_tiles, x2d, g.reshape(1, d), w_gate_up_bf16, w_gate_up_bf16, w_down_bf16)


META_E1, META_E2, META_W1, META_W2, META_R1, META_R2 = range(6)


def _router_kernel(x_ref, g_ref, wr_ref, meta_ref, cnt_ref, run_ref):
    @pl.when(pl.program_id(0) == 0)
    def _():
        run_ref[...] = jnp.zeros_like(run_ref)

    tm = x_ref.shape[0]
    xn = _rms(x_ref[...], g_ref[...])
    logits = jnp.dot(xn, wr_ref[...], preferred_element_type=F32, precision=lax.Precision.HIGHEST)
    lane = lax.broadcasted_iota(jnp.int32, logits.shape, 1)
    logits = jnp.where(lane < N_EXPERTS, logits, -jnp.inf)

    def top(vals):
        best = jnp.max(vals, axis=-1, keepdims=True)
        idx = jnp.min(jnp.where(vals == best, lane, LANES), axis=-1, keepdims=True)
        return best, idx

    v1, e1 = top(logits)
    v2, e2 = top(jnp.where(lane == e1, -jnp.inf, logits))
    ex = jnp.exp(v2 - v1)
    w1 = 1.0 / (1.0 + ex)
    w2 = ex / (1.0 + ex)

    oh1 = (lane == e1).astype(F32)
    oh2 = (lane == e2).astype(F32)
    rr = lax.broadcasted_iota(jnp.int32, (tm, tm), 0)
    cc = lax.broadcasted_iota(jnp.int32, (tm, tm), 1)
    strict = (cc < rr).astype(BF16)
    before = _dot(strict, (oh1 + oh2).astype(BF16)) + run_ref[0:1, :]
    r1 = jnp.sum(before * oh1, axis=-1, keepdims=True)
    r2 = jnp.sum(before * oh2, axis=-1, keepdims=True)
    run_ref[0:1, :] = run_ref[0:1, :] + jnp.sum(oh1 + oh2, axis=0, keepdims=True)

    meta = jnp.zeros(logits.shape, F32)
    for slot, val in ((META_E1, e1.astype(F32)), (META_E2, e2.astype(F32)), (META_W1, w1), (META_W2, w2),
                      (META_R1, r1), (META_R2, r2)):
        meta = jnp.where(lane == slot, val, meta)
    meta_ref[...] = meta
    cnt_ref[...] = run_ref[...]


def _router(x2d, g, w_router_padded, *, tm):
    t, d = x2d.shape
    return pl.pallas_call(
        _router_kernel,
        out_shape=[jax.ShapeDtypeStruct((t, LANES), F32), jax.ShapeDtypeStruct((8, LANES), F32)],
        grid=(t // tm,),
        in_specs=[
            pl.BlockSpec((tm, d), lambda i: (i, 0)),
            pl.BlockSpec((1, d), lambda i: (0, 0)),
            pl.BlockSpec((d, LANES), lambda i: (0, 0)),
        ],
        out_specs=[pl.BlockSpec((tm, LANES), lambda i: (i, 0)), pl.BlockSpec((8, LANES), lambda i: (0, 0))],
        scratch_shapes=[pltpu.VMEM((8, LANES), F32)],
        compiler_params=_params(("arbitrary",)),
        name="moe_router",
    )(x2d, g.reshape(1, d), w_router_padded)


ROW_DMA_UNROLL = 8


def _dispatch_kernel(slot_ref, x_ref, g_ref, zeros_ref, xs_ref, xn_ref, sem):
    del zeros_ref
    tm = x_ref.shape[0]
    xn_ref[...] = _rms(x_ref[...], g_ref[...])

    def row_copy(r, k):
        return pltpu.make_async_copy(xn_ref.at[pl.ds(r, 1)], xs_ref.at[pl.ds(slot_ref[0, 0, 2 * r + k], 1)], sem)

    def start(r, _):
        row_copy(r, 0).start()
        row_copy(r, 1).start()
        return 0

    def wait(r, _):
        row_copy(r, 0).wait()
        row_copy(r, 1).wait()
        return 0

    lax.fori_loop(0, tm, start, 0, unroll=ROW_DMA_UNROLL)
    lax.fori_loop(0, tm, wait, 0, unroll=ROW_DMA_UNROLL)


def _dispatch(slots3, x2d, g, n_slots, *, tm):
    t, d = x2d.shape
    return pl.pallas_call(
        _dispatch_kernel,
        out_shape=jax.ShapeDtypeStruct((n_slots, d), F32),
        grid=(t // tm,),
        in_specs=[
            pl.BlockSpec((1, 1, 2 * tm), lambda i: (i, 0, 0), memory_space=pltpu.SMEM),
            pl.BlockSpec((tm, d), lambda i: (i, 0)),
            pl.BlockSpec((1, d), lambda i: (0, 0)),
            pl.BlockSpec(memory_space=pl.ANY),
        ],
        out_specs=pl.BlockSpec(memory_space=pl.ANY),
        scratch_shapes=[pltpu.VMEM((tm, d), F32), pltpu.SemaphoreType.DMA],
        input_output_aliases={3: 0},
        compiler_params=_params(("arbitrary",)),
        name="moe_dispatch",
    )(slots3, x2d, g.reshape(1, d), jnp.zeros((n_slots, d), F32))


def _combine_kernel(slot_ref, x_ref, meta_ref, y_ref, g_ref, out_ref, ybuf_ref, sem, *, final_norm):
    tm = x_ref.shape[0]

    def row_copy(r, k):
        return pltpu.make_async_copy(y_ref.at[pl.ds(slot_ref[0, 0, 2 * r + k], 1)], ybuf_ref.at[k, pl.ds(r, 1)], sem)

    def start(r, _):
        row_copy(r, 0).start()
        row_copy(r, 1).start()
        return 0

    def wait(r, _):
        row_copy(r, 0).wait()
        row_copy(r, 1).wait()
        return 0

    lax.fori_loop(0, tm, start, 0, unroll=ROW_DMA_UNROLL)
    lax.fori_loop(0, tm, wait, 0, unroll=ROW_DMA_UNROLL)
    meta = meta_ref[...]
    w1 = meta[:, META_W1:META_W1 + 1]
    w2 = meta[:, META_W2:META_W2 + 1]
    out = x_ref[...] + (w1 * ybuf_ref[0] + w2 * ybuf_ref[1])
    out_ref[...] = _rms(out, g_ref[...]) if final_norm else out


def _combine(slots3, x2d, meta, y, g, *, tm, final_norm):
    t, d = x2d.shape
    return pl.pallas_call(
        functools.partial(_combine_kernel, final_norm=final_norm),
        out_shape=jax.ShapeDtypeStruct((t, d), F32),
        grid=(t // tm,),
        in_specs=[
            pl.BlockSpec((1, 1, 2 * tm), lambda i: (i, 0, 0), memory_space=pltpu.SMEM),
            pl.BlockSpec((tm, d), lambda i: (i, 0)),
            pl.BlockSpec((tm, LANES), lambda i: (i, 0)),
            pl.BlockSpec(memory_space=pl.ANY),
            pl.BlockSpec((1, d), lambda i: (0, 0)),
        ],
        out_specs=pl.BlockSpec((tm, d), lambda i: (i, 0)),
        scratch_shapes=[pltpu.VMEM((2, tm, d), F32), pltpu.SemaphoreType.DMA],
        compiler_params=_params(("arbitrary",)),
        name="moe_combine",
    )(slots3, x2d, meta, y, g.reshape(1, d))


def _moe(x2d, norm_g, w_router, w_gate_up, w_down, final_g, *, tm_route, tm_row, tm_ffn, tf, final_norm):
    t, d = x2d.shape
    wr = jnp.zeros((d, LANES), F32).at[:, :N_EXPERTS].set(w_router)
    meta, counts = _router(x2d, norm_g, wr, tm=tm_route)

    cnt = counts[0, :N_EXPERTS].astype(jnp.int32)
    padded = (cnt + tm_ffn - 1) // tm_ffn * tm_ffn
    ends = jnp.cumsum(padded)
    starts = ends - padded
    e12 = meta[:, META_E1:META_E2 + 1].astype(jnp.int32)
    r12 = meta[:, META_R1:META_R2 + 1].astype(jnp.int32)
    slots = starts[e12] + r12
    slots3 = slots.reshape(t // tm_row, 1, 2 * tm_row)
    n_tiles_max = (2 * t) // tm_ffn + N_EXPERTS
    tile_start = jnp.arange(n_tiles_max, dtype=jnp.int32) * tm_ffn
    tile_expert = jnp.minimum(
        jnp.sum((ends[None, :] <= tile_start[:, None]).astype(jnp.int32), axis=1), N_EXPERTS - 1)
    n_tiles = (ends[N_EXPERTS - 1] // tm_ffn).astype(jnp.int32).reshape(1)

    xs = _dispatch(slots3, x2d, norm_g, n_tiles_max * tm_ffn, tm=tm_row)
    ys = _ffn(tile_expert, n_tiles, xs, norm_g, w_gate_up.astype(BF16), w_down.astype(BF16),
              tm=tm_ffn, tf=tf, dense=False)
    return _combine(slots3, x2d, meta, ys, final_g, tm=tm_row, final_norm=final_norm)


def _tiles(batch, seq):
    full = seq >= 4096
    return dict(
        tm_proj=512 if full else 128,
        tq=256 if full else 128,
        ts=512 if full else 128,
        tm_out=512 if full else 128,
        tm_dense=512 if full else 128,
        tm_route=512 if full else 128,
        tm_row=256 if full else 128,
        tm_ffn=1024 if full else 128,
    )


def kernel(x, mem, a_norm_mix, a_w_in, a_lam_q1, a_lam_k1, a_lam_q2, a_lam_k2, a_subln, a_mem_norm, a_w_mem_kv, a_w_out, b_norm_mix, b_w_in, b_lb_logits, b_out_norm, b_mem_norm, b_w_mem_kv, b_w_out, dense_norm, dense_w_gate_up, dense_w_down, moe_norm, moe_router, moe_w_gate_up, moe_w_down, final_norm):
    batch, seq, d = x.shape
    mem_len = mem.shape[1]
    t = batch * seq
    ts = _tiles(batch, seq)
    x2d = x.reshape(t, d)
    mem2d = mem.reshape(batch * mem_len, d)
    kv_a = _memkv(mem2d, a_mem_norm, a_w_mem_kv, mem_len)
    kv_b = _memkv(mem2d, b_mem_norm, b_w_mem_kv, mem_len)
    one_tile = jnp.zeros((t // ts["tm_dense"],), jnp.int32)
    all_tiles = jnp.full((1,), t // ts["tm_dense"], jnp.int32)
    d_ff = dense_w_down.shape[1]
    tf_dense = d_ff // 2 if (d_ff // 2) % LANES == 0 else d_ff
    tf_moe = 512 if moe_w_down.shape[2] % 512 == 0 else moe_w_down.shape[2]

    for i in range(DEPTH):
        j = i // N_MIXERS
        if i % N_MIXERS == 0:
            lam_init = 0.8 - 0.6 * float(np.exp(-0.3 * i))
            proj = _normproj(x2d, a_norm_mix[j], a_w_in[j].astype(BF16), tm=ts["tm_proj"],
                             scaled_cols=DA_WIDTH, col_scale=DA_HEAD_DIM ** -0.5 * LOG2E)
            lamv = jnp.stack([a_lam_q1[j], a_lam_k1[j], a_lam_q2[j], a_lam_k2[j]])
            o = _diff_attention(proj, lamv, a_subln[j], batch=batch, seq=seq, tq=ts["tq"], lam_init=lam_init)
            x2d = _outproj(o, proj, 3 * DA_WIDTH // MEM_WIDTH, kv_a, j, x2d, a_w_out[j].astype(BF16),
                           tm=ts["tm_out"], seq=seq, mem_len=mem_len)
        else:
            proj, fz32 = _normproj(x2d, b_norm_mix[j], b_w_in[j].astype(BF16), tm=ts["tm_proj"],
                                   f32_cols=(HG_KEY_WIDTH, 2 * HG_KEY_WIDTH))
            o = _hgrn(proj, fz32, b_lb_logits, b_out_norm[j], batch=batch, seq=seq, ts=ts["ts"], layer=j)
            x2d = _outproj(o, proj, (2 * HG_KEY_WIDTH + 2 * HG_VAL_WIDTH) // MEM_WIDTH, kv_b, j, x2d,
                           b_w_out[j].astype(BF16), tm=ts["tm_out"], seq=seq, mem_len=mem_len)
        if i % 2 == 0:
            x2d = _ffn(one_tile, all_tiles, x2d, dense_norm[j], dense_w_gate_up[j].astype(BF16)[None],
                       dense_w_down[j].astype(BF16)[None], tm=ts["tm_dense"], tf=tf_dense, dense=True)
        else:
            x2d = _moe(x2d, moe_norm[j], moe_router[j], moe_w_gate_up[j], moe_w_down[j], final_norm,
                       tm_route=ts["tm_route"], tm_row=ts["tm_row"], tm_ffn=ts["tm_ffn"], tf=tf_moe,
                       final_norm=(i == DEPTH - 1))
    return x2d.reshape(batch, seq, d)
```

```python
import functools

import numpy as np
import jax
import jax.numpy as jnp
from jax import lax
from jax.experimental import pallas as pl
from jax.experimental.pallas import tpu as pltpu

F32 = jnp.float32
BF16 = jnp.bfloat16

DEPTH = 4
N_MIXERS = 2
CHUNK = 64
DA_HEADS = 6
DA_HEAD_DIM = 64
DA_WIDTH = DA_HEADS * 2 * DA_HEAD_DIM
HG_HEADS = 6
HG_KEY_DIM = 128
HG_VAL_DIM = 128
HG_KEY_WIDTH = HG_HEADS * HG_KEY_DIM
HG_VAL_WIDTH = HG_HEADS * HG_VAL_DIM
MEM_HEADS = 4
MEM_HEAD_DIM = 64
MEM_WIDTH = MEM_HEADS * MEM_HEAD_DIM
N_EXPERTS = 8
EPS = 1e-6
MASK_VALUE = -1e30
LOG2E = float(np.log2(np.e))

LANES = 128
SUB_BLOCK = 16
HG_SAFE_LOG_DECAY = -64.0
VMEM_LIMIT = 56 * 1024 * 1024


def _params(sem):
    return pltpu.CompilerParams(dimension_semantics=sem, vmem_limit_bytes=VMEM_LIMIT)


def _rms(x, g):
    return x * lax.rsqrt(jnp.mean(x * x, axis=-1, keepdims=True) + EPS) * g


def _dot(a, b):
    return jnp.dot(a, b, preferred_element_type=F32)


def _dot_nt(a, b):
    return lax.dot_general(a, b, (((1,), (1,)), ((), ())), preferred_element_type=F32)


def _memkv_kernel(mem_ref, g_ref, w_ref, out_ref):
    y = _rms(mem_ref[...], g_ref[0]).astype(BF16)
    out_ref[0] = _dot(y, w_ref[0].astype(BF16)).astype(BF16)


def _memkv(mem2d, g, w, mem_len):
    n_layers, d, n = w.shape
    rows = mem2d.shape[0]
    return pl.pallas_call(
        _memkv_kernel,
        out_shape=jax.ShapeDtypeStruct((n_layers, rows, n), BF16),
        grid=(n_layers, rows // mem_len),
        in_specs=[
            pl.BlockSpec((mem_len, d), lambda l, b: (b, 0)),
            pl.BlockSpec((1, 1, d), lambda l, b: (l, 0, 0)),
            pl.BlockSpec((1, d, n), lambda l, b: (l, 0, 0)),
        ],
        out_specs=pl.BlockSpec((1, mem_len, n), lambda l, b: (l, b, 0)),
        compiler_params=_params(("arbitrary", "arbitrary")),
        name="memkv",
    )(mem2d, g.reshape(n_layers, 1, d), w)


NP_CHUNK = 256


def _normproj_kernel(x_ref, g_ref, w_ref, out_ref, *f32_refs, scaled_cols, col_scale, f32_cols):
    xn = _rms(x_ref[...], g_ref[...]).astype(BF16)
    n = w_ref.shape[1]
    for c0 in range(0, n, NP_CHUNK):
        y = _dot(xn, w_ref[:, c0:c0 + NP_CHUNK])
        if c0 < scaled_cols:
            y = y * col_scale
        out_ref[:, c0:c0 + NP_CHUNK] = y.astype(BF16)
        if f32_cols is not None and f32_cols[0] <= c0 < f32_cols[1]:
            f32_refs[0][:, c0 - f32_cols[0]:c0 - f32_cols[0] + NP_CHUNK] = y


def _normproj(x2d, g, w_bf16, *, tm, scaled_cols=0, col_scale=1.0, f32_cols=None):
    t, d = x2d.shape
    n = w_bf16.shape[1]
    assert n % NP_CHUNK == 0 and scaled_cols % NP_CHUNK == 0
    out_shape = [jax.ShapeDtypeStruct((t, n), BF16)]
    out_specs = [pl.BlockSpec((tm, n), lambda i: (i, 0))]
    if f32_cols is not None:
        assert f32_cols[0] % NP_CHUNK == 0 and f32_cols[1] % NP_CHUNK == 0
        wf = f32_cols[1] - f32_cols[0]
        out_shape.append(jax.ShapeDtypeStruct((t, wf), F32))
        out_specs.append(pl.BlockSpec((tm, wf), lambda i: (i, 0)))
    outs = pl.pallas_call(
        functools.partial(_normproj_kernel, scaled_cols=scaled_cols, col_scale=col_scale, f32_cols=f32_cols),
        out_shape=out_shape,
        grid=(t // tm,),
        in_specs=[
            pl.BlockSpec((tm, d), lambda i: (i, 0)),
            pl.BlockSpec((1, d), lambda i: (0, 0)),
            pl.BlockSpec((d, n), lambda i: (0, 0)),
        ],
        out_specs=out_specs,
        compiler_params=_params(("parallel",)),
        name="normproj",
    )(x2d, g.reshape(1, d), w_bf16)
    return outs if f32_cols is not None else outs[0]


ATT_ROWS = 32


def _fold8(x, op):
    out = x[0:8]
    for r in range(8, x.shape[0], 8):
        out = op(out, x[r:r + 8])
    return out


def _attn_kernel(dec_ref, q_ref, k_ref, v_ref, bias_ref, lamv_ref, g_ref, o_ref,
                 vt_ref, s0_ref, s1_ref, mx0_ref, mx1_ref, p_ref, st_ref, acc_ref, *, tq, lam_init):
    h = pl.program_id(1)
    nq = q_ref.shape[0] // tq
    n_pairs = nq * (nq + 1) // 2
    assert n_pairs % 2 == 0
    s_refs, mx_refs = (s0_ref, s1_ref), (mx0_ref, mx1_ref)

    for j in range(nq):
        vt_ref[j] = v_ref[j * tq:(j + 1) * tq, :].astype(F32).T.astype(BF16)
    for ref in (s1_ref, mx1_ref, acc_ref, st_ref):
        ref[...] = jnp.zeros_like(ref)

    dec = dec_ref[h]
    lo_map = lax.broadcasted_iota(jnp.int32, (tq, LANES), 1) < DA_HEAD_DIM
    lv = lamv_ref[...]
    lam = (jnp.exp(jnp.sum(lv[0:1] * lv[1:2], axis=-1, keepdims=True))
           - jnp.exp(jnp.sum(lv[2:3] * lv[3:4], axis=-1, keepdims=True)) + lam_init)
    out_gain = g_ref[...] * (1.0 - lam_init)

    def step(slot, pair_a, pair_b):
        (qa, ka), (qb, kb) = pair_a, pair_b
        s_a, s_b = s_refs[slot], s_refs[1 - slot]

        q = q_ref[pl.ds(pl.multiple_of(qa * tq, tq), tq), :]
        kt = k_ref[pl.ds(pl.multiple_of(ka * tq, tq), tq), :]
        zero = jnp.zeros_like(q)
        bias = bias_ref[0, (ka == qa).astype(jnp.int32)]
        for mi, qz in enumerate((jnp.where(lo_map, q, zero), jnp.where(lo_map, zero, q))):
            s = _dot_nt(kt, qz) + bias
            s_a[mi] = s
            mx_refs[slot][mi] = _fold8(s, jnp.maximum)

        first = kb == 0
        vt = vt_ref[kb]
        for mi in range(2):
            m_old = jnp.where(first, MASK_VALUE, st_ref[mi, 0:1, :])
            l_old = jnp.where(first, 0.0, st_ref[mi, 1:2, :])
            mp = m_old - dec
            mn = jnp.maximum(mp, jnp.max(mx_refs[1 - slot][mi], axis=0, keepdims=True))
            alpha = jnp.exp2(mp - mn)
            l8 = None
            for c in range(0, tq, ATT_ROWS):
                p = jnp.exp2(s_b[mi, c:c + ATT_ROWS, :] - mn)
                p_ref[mi, c:c + ATT_ROWS, :] = p.astype(BF16)
                cs = _fold8(p, jnp.add)
                l8 = cs if l8 is None else l8 + cs
            st_ref[mi, 0:1, :] = mn
            st_ref[mi, 1:2, :] = alpha * l_old + jnp.sum(l8, axis=0, keepdims=True)
            kept = jnp.where(first, 0.0, alpha * acc_ref[mi])
            acc_ref[mi] = kept + _dot(vt, p_ref[mi])

        @pl.when(kb == qb)
        def _():
            ot = acc_ref[0] / st_ref[0, 1:2, :] - lam * (acc_ref[1] / st_ref[1, 1:2, :])
            ot = ot * lax.rsqrt(jnp.mean(ot * ot, axis=0, keepdims=True) + EPS) * out_gain
            o_ref[pl.ds(pl.multiple_of(qb * tq, tq), tq), :] = ot.T.astype(BF16)

    def advance(pair):
        qx, kx = pair
        row_done = kx == qx
        last = jnp.logical_and(row_done, qx == nq - 1)
        return (jnp.where(jnp.logical_and(row_done, jnp.logical_not(last)), qx + 1, qx),
                jnp.where(last, kx, jnp.where(row_done, 0, kx + 1)))

    def body(_, pairs):
        pa, pb = pairs
        step(0, pa, pb)
        pa2 = advance(pa)
        step(1, pa2, pa)
        return (advance(pa2), pa2)

    origin = (jnp.int32(0), jnp.int32(0))
    pa, pb = lax.fori_loop(0, n_pairs // 2, body, (origin, origin))
    step(0, pa, pb)


def _alibi_tiles(tq):
    slopes = 2.0 ** (-8.0 * np.arange(1, DA_HEADS + 1) / DA_HEADS)
    kpos = np.arange(tq)[:, None]
    qpos = np.arange(tq)[None, :]
    low = slopes[:, None, None] * LOG2E * (kpos - qpos)[None]
    diag = np.where((kpos // CHUNK <= qpos // CHUNK)[None],
                    -slopes[:, None, None] * LOG2E * np.abs(qpos - kpos)[None], MASK_VALUE)
    dec = slopes * LOG2E * tq
    return jnp.asarray(dec, F32), jnp.asarray(np.stack([low, diag], axis=1), F32)


def _diff_attention(proj, lamv, subln_g, *, batch, seq, tq, lam_init):
    t = proj.shape[0]
    nq = seq // tq
    dec, bias = _alibi_tiles(tq)
    grid_spec = pltpu.PrefetchScalarGridSpec(
        num_scalar_prefetch=1,
        grid=(batch, DA_HEADS),
        in_specs=[
            pl.BlockSpec((seq, LANES), lambda b, h, dec: (b, h)),
            pl.BlockSpec((seq, LANES), lambda b, h, dec: (b, DA_HEADS + h)),
            pl.BlockSpec((seq, LANES), lambda b, h, dec: (b, 2 * DA_HEADS + h)),
            pl.BlockSpec((1, 2, tq, tq), lambda b, h, dec: (h, 0, 0, 0)),
            pl.BlockSpec((4, DA_HEAD_DIM), lambda b, h, dec: (0, 0)),
            pl.BlockSpec((LANES, 1), lambda b, h, dec: (0, 0)),
        ],
        out_specs=pl.BlockSpec((seq, LANES), lambda b, h, dec: (b, h)),
        scratch_shapes=[
            pltpu.VMEM((nq, LANES, tq), BF16),
            pltpu.VMEM((2, tq, tq), F32),
            pltpu.VMEM((2, tq, tq), F32),
            pltpu.VMEM((2, 8, tq), F32),
            pltpu.VMEM((2, 8, tq), F32),
            pltpu.VMEM((2, tq, tq), BF16),
            pltpu.VMEM((2, 8, tq), F32),
            pltpu.VMEM((2, LANES, tq), F32),
        ],
    )
    return pl.pallas_call(
        functools.partial(_attn_kernel, tq=tq, lam_init=lam_init),
        out_shape=jax.ShapeDtypeStruct((t, DA_WIDTH), BF16),
        grid_spec=grid_spec,
        compiler_params=_params(("parallel", "parallel")),
        name="diff_attn",
    )(dec, proj, proj, proj, bias, lamv, subln_g.reshape(LANES, 1))


def _hgrn_kernel(q_ref, fz_ref, v_ref, gate_ref, lbl_ref, g_ref, o_ref, st_ref, *, layer, n_chunks):
    @pl.when(pl.program_id(2) == 0)
    def _():
        st_ref[...] = jnp.zeros_like(st_ref)

    lbl = lbl_ref[...]
    e = jnp.exp(lbl - jnp.max(lbl, axis=0, keepdims=True))
    pr = e / jnp.sum(e, axis=0, keepdims=True)
    lb = jnp.sum(pr[0:layer + 1], axis=0, keepdims=True) - pr[0:1]

    rr = lax.broadcasted_iota(jnp.int32, (CHUNK, CHUNK), 0)
    cc = lax.broadcasted_iota(jnp.int32, (CHUNK, CHUNK), 1)
    tril = (cc <= rr).astype(F32)
    sub_row = lax.broadcasted_iota(jnp.int32, (SUB_BLOCK, LANES), 0)
    n_sub = CHUNK // SUB_BLOCK
    gain = g_ref[...]

    z_all = fz_ref[...]
    f_all = lb + (1.0 - lb) * (1.0 / (1.0 + jnp.exp(-z_all)))
    k_all = (1.0 - lb) * (1.0 / (1.0 + jnp.exp(z_all)))
    logf = jnp.log(f_all)
    lf_hi = logf.astype(BF16)
    lf_r1 = logf - lf_hi.astype(F32)
    lf_mid = lf_r1.astype(BF16)
    lf_lo = (lf_r1 - lf_mid.astype(F32)).astype(BF16)
    lf3 = jnp.concatenate([lf_hi, lf_mid, lf_lo], axis=1)
    r2 = lax.broadcasted_iota(jnp.int32, (2 * CHUNK, CHUNK), 0)
    c2 = lax.broadcasted_iota(jnp.int32, (2 * CHUNK, CHUNK), 1)
    in_chunk = jnp.logical_and(r2 < CHUNK, c2 <= r2)
    in_sub = jnp.logical_and(jnp.logical_and(r2 >= CHUNK, c2 <= r2 - CHUNK),
                             c2 // SUB_BLOCK == (r2 - CHUNK) // SUB_BLOCK)
    cum_w = jnp.where(jnp.logical_or(in_chunk, in_sub), 1.0, 0.0).astype(BF16)
    cums = []
    for c in range(n_chunks):
        e3 = _dot(cum_w, lf3[c * CHUNK:(c + 1) * CHUNK])
        cums.append(e3[:, 0:LANES] + e3[:, LANES:2 * LANES] + e3[:, 2 * LANES:3 * LANES])
    min_rel = cums[0][CHUNK:]
    for c in range(1, n_chunks):
        min_rel = jnp.minimum(min_rel, cums[c][CHUNK:])
    safe = jnp.min(min_rel) >= HG_SAFE_LOG_DECAY

    blk_r, blk_c = rr // SUB_BLOCK, cc // SUB_BLOCK
    mask_sub = jnp.logical_and(blk_r == blk_c, cc <= rr)
    mask_16 = jnp.logical_and(blk_r % 2 == 1, blk_c == blk_r - 1)
    zeros16 = jnp.zeros((SUB_BLOCK, LANES), F32)
    zeros32 = jnp.zeros((2 * SUB_BLOCK, LANES), F32)

    @pl.when(safe)
    def _():
        st = st_ref[...]
        for c in range(n_chunks):
            rows = slice(c * CHUNK, (c + 1) * CHUNK)
            q = q_ref[rows, :].astype(F32)
            vb = v_ref[rows, :]
            k = k_all[rows]
            b, rel = cums[c][:CHUNK], cums[c][CHUNK:]
            b15, b31, b47, b63 = b[15:16], b[31:32], b[47:48], b[63:64]
            q32 = jnp.concatenate([zeros32, q[32:64] * jnp.exp(b[32:64] - b31)], axis=0)
            k32 = jnp.concatenate([k[0:32] * jnp.exp(b31 - b[0:32]), zeros32], axis=0)
            q16 = jnp.concatenate([zeros16, q[16:32] * jnp.exp(b[16:32] - b15),
                                   zeros16, q[48:64] * jnp.exp(b[48:64] - b47)], axis=0)
            k16 = jnp.concatenate([k[0:16] * jnp.exp(b15 - b[0:16]), zeros16,
                                   k[32:48] * jnp.exp(b47 - b[32:48]), zeros16], axis=0)
            attn = (_dot_nt(q32.astype(BF16), k32.astype(BF16))
                    + jnp.where(mask_16, _dot_nt(q16.astype(BF16), k16.astype(BF16)), 0.0)
                    + jnp.where(mask_sub, _dot_nt((q * jnp.exp(rel)).astype(BF16),
                                                  (k * jnp.exp(-rel)).astype(BF16)), 0.0))
            o = _dot(attn.astype(BF16), vb) + _dot_nt((q * jnp.exp(b)).astype(BF16), st.astype(BF16))
            upd = lax.dot_general(vb, (k * jnp.exp(b63 - b)).astype(BF16), (((0,), (0,)), ((), ())),
                                  preferred_element_type=F32)
            st = st * jnp.exp(b63) + upd
            gt = gate_ref[rows, :].astype(F32)
            o_ref[rows, :] = (_rms(o, gain) * (gt * (1.0 / (1.0 + jnp.exp(-gt))))).astype(BF16)
        st_ref[...] = st

    def chunk(c, _):
        rows = pl.ds(pl.multiple_of(c * CHUNK, CHUNK), CHUNK)
        z = fz_ref[rows, :]
        q = q_ref[rows, :].astype(F32)
        v = v_ref[rows, :].astype(F32)
        f = lb + (1.0 - lb) * (1.0 / (1.0 + jnp.exp(-z)))
        k = (1.0 - lb) * (1.0 / (1.0 + jnp.exp(z)))
        b = jnp.dot(tril, jnp.log(f), preferred_element_type=F32, precision=lax.Precision.HIGHEST)
        vb = v.astype(BF16)
        st = st_ref[...]
        o_inter = _dot_nt((q * jnp.exp(b)).astype(BF16), st.astype(BF16))

        outs = []
        for i in range(n_sub):
            r0 = i * SUB_BLOCK
            qi = q[r0:r0 + SUB_BLOCK]
            ki = k[r0:r0 + SUB_BLOCK]
            bi = b[r0:r0 + SUB_BLOCK]
            vi = v[r0:r0 + SUB_BLOCK]
            oi = jnp.zeros((SUB_BLOCK, LANES), F32)
            for s in range(SUB_BLOCK):
                dcy = jnp.where(sub_row >= s, jnp.exp(bi - bi[s:s + 1]), 0.0)
                a_s = jnp.sum(dcy * qi * ki[s:s + 1], axis=-1, keepdims=True)
                oi = oi + a_s * vi[s:s + 1]
            if i > 0:
                ref_b = b[r0 - 1:r0]
                qd = (qi * jnp.exp(bi - ref_b)).astype(BF16)
                kd = (k[0:r0] * jnp.exp(ref_b - b[0:r0])).astype(BF16)
                oi = oi + _dot(_dot_nt(qd, kd).astype(BF16), vb[0:r0])
            outs.append(oi)
        o = jnp.concatenate(outs, axis=0) + o_inter

        b_last = b[CHUNK - 1:CHUNK]
        kd = (k * jnp.exp(b_last - b)).astype(BF16)
        upd = lax.dot_general(vb, kd, (((0,), (0,)), ((), ())), preferred_element_type=F32)
        st_ref[...] = st * jnp.exp(b_last) + upd

        gt = gate_ref[rows, :].astype(F32)
        o_ref[rows, :] = (_rms(o, gain) * (gt * (1.0 / (1.0 + jnp.exp(-gt))))).astype(BF16)
        return 0

    @pl.when(jnp.logical_not(safe))
    def _():
        lax.fori_loop(0, n_chunks, chunk, 0)


def _hgrn(proj, fz32, lb_logits, out_norm_g, *, batch, seq, ts, layer):
    t = proj.shape[0]
    ns = seq // ts
    n_b = lb_logits.shape[0]
    return pl.pallas_call(
        functools.partial(_hgrn_kernel, layer=layer, n_chunks=ts // CHUNK),
        out_shape=jax.ShapeDtypeStruct((t, HG_VAL_WIDTH), BF16),
        grid=(batch, HG_HEADS, ns),
        in_specs=[
            pl.BlockSpec((ts, LANES), lambda b, h, i: (b * ns + i, h)),
            pl.BlockSpec((ts, LANES), lambda b, h, i: (b * ns + i, h)),
            pl.BlockSpec((ts, LANES), lambda b, h, i: (b * ns + i, 2 * HG_HEADS + h)),
            pl.BlockSpec((ts, LANES), lambda b, h, i: (b * ns + i, 3 * HG_HEADS + h)),
            pl.BlockSpec((n_b, LANES), lambda b, h, i: (0, h)),
            pl.BlockSpec((1, LANES), lambda b, h, i: (0, 0)),
        ],
        out_specs=pl.BlockSpec((ts, LANES), lambda b, h, i: (b * ns + i, h)),
        scratch_shapes=[pltpu.VMEM((HG_VAL_DIM, HG_KEY_DIM), F32)],
        compiler_params=_params(("parallel", "parallel", "arbitrary")),
        name="hgrn2",
    )(proj, fz32, proj, proj, lb_logits, out_norm_g.reshape(1, LANES))


def _outproj_kernel(o_ref, qm_ref, kv_ref, x_ref, w_ref, out_ref):
    qm = qm_ref[...]
    kmem = kv_ref[0, :, 0:MEM_WIDTH]
    vmem = kv_ref[0, :, MEM_WIDTH:2 * MEM_WIDTH]
    head = lax.broadcasted_iota(jnp.int32, qm.shape, 1) // MEM_HEAD_DIM
    om = jnp.zeros(qm.shape, F32)
    for hh in range(MEM_HEADS):
        qh = jnp.where(head == hh, qm, jnp.zeros_like(qm))
        s = _dot_nt(qh, kmem) * (MEM_HEAD_DIM ** -0.5)
        p = jnp.exp(s - jnp.max(s, axis=-1, keepdims=True))
        p = p / jnp.sum(p, axis=-1, keepdims=True)
        om = jnp.where(head == hh, _dot(p.astype(BF16), vmem), om)
    wo = o_ref.shape[1]
    out_ref[...] = (x_ref[...] + _dot(o_ref[...], w_ref[0:wo, :])
                    + _dot(om.astype(BF16), w_ref[wo:wo + MEM_WIDTH, :]))


def _outproj(o, proj, qm_block, memkv, layer, x2d, w_out_bf16, *, tm, seq, mem_len):
    t, d = x2d.shape
    wo = o.shape[1]
    tiles_per_batch = seq // tm
    return pl.pallas_call(
        _outproj_kernel,
        out_shape=jax.ShapeDtypeStruct((t, d), F32),
        grid=(t // tm,),
        in_specs=[
            pl.BlockSpec((tm, wo), lambda i: (i, 0)),
            pl.BlockSpec((tm, MEM_WIDTH), lambda i: (i, qm_block)),
            pl.BlockSpec((1, mem_len, 2 * MEM_WIDTH), lambda i: (layer, i // tiles_per_batch, 0)),
            pl.BlockSpec((tm, d), lambda i: (i, 0)),
            pl.BlockSpec((wo + MEM_WIDTH, d), lambda i: (0, 0)),
        ],
        out_specs=pl.BlockSpec((tm, d), lambda i: (i, 0)),
        compiler_params=_params(("parallel",)),
        name="outproj",
    )(o, proj, memkv, x2d, w_out_bf16)


MXU_COLS = 256


def _ffn_kernel(te_ref, nt_ref, x_ref, g_ref, wg_ref, wu_ref, wd_ref, out_ref, *, dense):
    del te_ref
    i = pl.program_id(0)
    f = wd_ref.shape[1]
    half = f // 2 // MXU_COLS * MXU_COLS

    @pl.when(i < nt_ref[0])
    def _():
        x = x_ref[...]
        xn = (_rms(x, g_ref[...]) if dense else x).astype(BF16)
        y = x if dense else None
        for c0, c1 in ((0, half), (half, f)):
            gt = _dot(xn, wg_ref[0, :, c0:c1])
            up = _dot(xn, wu_ref[0, :, c0:c1])
            act = (gt * (1.0 / (1.0 + jnp.exp(-gt))) * up).astype(BF16)
            part = _dot(act, wd_ref[0, c0:c1, :])
            y = part if y is None else y + part
        out_ref[...] = y

    @pl.when(i >= nt_ref[0])
    def _():
        out_ref[...] = jnp.zeros_like(out_ref)


def _ffn(tile_expert, n_tiles, x2d, g, w_gate_up_bf16, w_down_bf16, *, tm, dense):
    r, d = x2d.shape
    f = w_down_bf16.shape[1]
    resident = pl.Buffered(1)
    grid_spec = pltpu.PrefetchScalarGridSpec(
        num_scalar_prefetch=2,
        grid=(r // tm,),
        in_specs=[
            pl.BlockSpec((tm, d), lambda i, te, nt: (i, 0)),
            pl.BlockSpec((1, d), lambda i, te, nt: (0, 0)),
            pl.BlockSpec((1, d, f), lambda i, te, nt: (te[i], 0, 0), pipeline_mode=resident),
            pl.BlockSpec((1, d, f), lambda i, te, nt: (te[i], 0, 1), pipeline_mode=resident),
            pl.BlockSpec((1, f, d), lambda i, te, nt: (te[i], 0, 0), pipeline_mode=resident),
        ],
        out_specs=pl.BlockSpec((tm, d), lambda i, te, nt: (i, 0)),
    )
    return pl.pallas_call(
        functools.partial(_ffn_kernel, dense=dense),
        out_shape=jax.ShapeDtypeStruct((r, d), F32),
        grid_spec=grid_spec,
        compiler_params=_params(("parallel",)),
        name="swiglu_dense" if dense else "swiglu_grouped",
    )(tile_expert, n_tiles, x2d, g.reshape(1, d), w_gate_up_bf16, w_gate_up_bf16, w_down_bf16)


META_E1, META_E2, META_W1, META_W2, META_R1, META_R2 = range(6)


def _router_kernel(x_ref, g_ref, wr_ref, meta_ref, cnt_ref, run_ref):
    @pl.when(pl.program_id(0) == 0)
    def _():
        run_ref[...] = jnp.zeros_like(run_ref)

    tm = x_ref.shape[0]
    xn = _rms(x_ref[...], g_ref[...])
    x_hi = xn.astype(BF16)
    x_lo = (xn - x_hi.astype(F32)).astype(BF16)
    w = wr_ref[...]
    w_hi = w.astype(BF16)
    w_lo = (w - w_hi.astype(F32)).astype(BF16)
    logits = _dot(x_hi, w_hi) + (_dot(x_lo, w_hi) + _dot(x_hi, w_lo))
    lane = lax.broadcasted_iota(jnp.int32, logits.shape, 1)
    logits = jnp.where(lane < N_EXPERTS, logits, -jnp.inf)

    def top(vals):
        best = jnp.max(vals, axis=-1, keepdims=True)
        idx = jnp.min(jnp.where(vals == best, lane, LANES), axis=-1, keepdims=True)
        return best, idx

    v1, e1 = top(logits)
    v2, e2 = top(jnp.where(lane == e1, -jnp.inf, logits))
    ex = jnp.exp(v2 - v1)
    w1 = 1.0 / (1.0 + ex)
    w2 = ex / (1.0 + ex)

    oh1 = (lane == e1).astype(F32)
    oh2 = (lane == e2).astype(F32)
    rr = lax.broadcasted_iota(jnp.int32, (tm, tm), 0)
    cc = lax.broadcasted_iota(jnp.int32, (tm, tm), 1)
    strict = (cc < rr).astype(BF16)
    before = _dot(strict, (oh1 + oh2).astype(BF16)) + run_ref[0:1, :]
    r1 = jnp.sum(before * oh1, axis=-1, keepdims=True)
    r2 = jnp.sum(before * oh2, axis=-1, keepdims=True)
    run_ref[0:1, :] = run_ref[0:1, :] + jnp.sum(oh1 + oh2, axis=0, keepdims=True)

    meta = jnp.zeros(logits.shape, F32)
    for slot, val in ((META_E1, e1.astype(F32)), (META_E2, e2.astype(F32)), (META_W1, w1), (META_W2, w2),
                      (META_R1, r1), (META_R2, r2)):
        meta = jnp.where(lane == slot, val, meta)
    meta_ref[...] = meta
    cnt_ref[...] = run_ref[...]


def _router(x2d, g, w_router_padded, *, tm):
    t, d = x2d.shape
    return pl.pallas_call(
        _router_kernel,
        out_shape=[jax.ShapeDtypeStruct((t, LANES), F32), jax.ShapeDtypeStruct((8, LANES), F32)],
        grid=(t // tm,),
        in_specs=[
            pl.BlockSpec((tm, d), lambda i: (i, 0)),
            pl.BlockSpec((1, d), lambda i: (0, 0)),
            pl.BlockSpec((d, LANES), lambda i: (0, 0)),
        ],
        out_specs=[pl.BlockSpec((tm, LANES), lambda i: (i, 0)), pl.BlockSpec((8, LANES), lambda i: (0, 0))],
        scratch_shapes=[pltpu.VMEM((8, LANES), F32)],
        compiler_params=_params(("arbitrary",)),
        name="moe_router",
    )(x2d, g.reshape(1, d), w_router_padded)


ROW_DMA_UNROLL = 8


def _dispatch_kernel(slot_ref, x_ref, g_ref, zeros_ref, xs_ref, xn_ref, sem):
    del zeros_ref
    tm = x_ref.shape[0]
    xn_ref[...] = _rms(x_ref[...], g_ref[...])

    def row_copy(r, k):
        return pltpu.make_async_copy(xn_ref.at[pl.ds(r, 1)], xs_ref.at[pl.ds(slot_ref[0, 0, 2 * r + k], 1)], sem)

    def start(r, _):
        row_copy(r, 0).start()
        row_copy(r, 1).start()
        return 0

    def wait(r, _):
        row_copy(r, 0).wait()
        row_copy(r, 1).wait()
        return 0

    lax.fori_loop(0, tm, start, 0, unroll=ROW_DMA_UNROLL)
    lax.fori_loop(0, tm, wait, 0, unroll=ROW_DMA_UNROLL)


def _dispatch(slots3, x2d, g, n_slots, *, tm):
    t, d = x2d.shape
    return pl.pallas_call(
        _dispatch_kernel,
        out_shape=jax.ShapeDtypeStruct((n_slots, d), F32),
        grid=(t // tm,),
        in_specs=[
            pl.BlockSpec((1, 1, 2 * tm), lambda i: (i, 0, 0), memory_space=pltpu.SMEM),
            pl.BlockSpec((tm, d), lambda i: (i, 0)),
            pl.BlockSpec((1, d), lambda i: (0, 0)),
            pl.BlockSpec(memory_space=pl.ANY),
        ],
        out_specs=pl.BlockSpec(memory_space=pl.ANY),
        scratch_shapes=[pltpu.VMEM((tm, d), F32), pltpu.SemaphoreType.DMA],
        input_output_aliases={3: 0},
        compiler_params=_params(("arbitrary",)),
        name="moe_dispatch",
    )(slots3, x2d, g.reshape(1, d), jnp.zeros((n_slots, d), F32))


def _combine_kernel(slot_ref, x_ref, meta_ref, y_ref, g_ref, out_ref, ybuf_ref, sem, *, final_norm):
    tm = x_ref.shape[0]

    def row_copy(r, k):
        return pltpu.make_async_copy(y_ref.at[pl.ds(slot_ref[0, 0, 2 * r + k], 1)], ybuf_ref.at[k, pl.ds(r, 1)], sem)

    def start(r, _):
        row_copy(r, 0).start()
        row_copy(r, 1).start()
        return 0

    def wait(r, _):
        row_copy(r, 0).wait()
        row_copy(r, 1).wait()
        return 0

    lax.fori_loop(0, tm, start, 0, unroll=ROW_DMA_UNROLL)
    lax.fori_loop(0, tm, wait, 0, unroll=ROW_DMA_UNROLL)
    meta = meta_ref[...]
    w1 = meta[:, META_W1:META_W1 + 1]
    w2 = meta[:, META_W2:META_W2 + 1]
    out = x_ref[...] + (w1 * ybuf_ref[0] + w2 * ybuf_ref[1])
    out_ref[...] = _rms(out, g_ref[...]) if final_norm else out


def _combine(slots3, x2d, meta, y, g, *, tm, final_norm):
    t, d = x2d.shape
    return pl.pallas_call(
        functools.partial(_combine_kernel, final_norm=final_norm),
        out_shape=jax.ShapeDtypeStruct((t, d), F32),
        grid=(t // tm,),
        in_specs=[
            pl.BlockSpec((1, 1, 2 * tm), lambda i: (i, 0, 0), memory_space=pltpu.SMEM),
            pl.BlockSpec((tm, d), lambda i: (i, 0)),
            pl.BlockSpec((tm, LANES), lambda i: (i, 0)),
            pl.BlockSpec(memory_space=pl.ANY),
            pl.BlockSpec((1, d), lambda i: (0, 0)),
        ],
        out_specs=pl.BlockSpec((tm, d), lambda i: (i, 0)),
        scratch_shapes=[pltpu.VMEM((2, tm, d), F32), pltpu.SemaphoreType.DMA],
        compiler_params=_params(("arbitrary",)),
        name="moe_combine",
    )(slots3, x2d, meta, y, g.reshape(1, d))


def _moe(x2d, norm_g, w_router, w_gate_up, w_down, final_g, *, tm_route, tm_row, tm_ffn, final_norm):
    t, d = x2d.shape
    wr = jnp.zeros((d, LANES), F32).at[:, :N_EXPERTS].set(w_router)
    meta, counts = _router(x2d, norm_g, wr, tm=tm_route)

    cnt = counts[0, :N_EXPERTS].astype(jnp.int32)
    padded = (cnt + tm_ffn - 1) // tm_ffn * tm_ffn
    ends = jnp.cumsum(padded)
    starts = ends - padded
    e12 = meta[:, META_E1:META_E2 + 1].astype(jnp.int32)
    r12 = meta[:, META_R1:META_R2 + 1].astype(jnp.int32)
    slots = starts[e12] + r12
    slots3 = slots.reshape(t // tm_row, 1, 2 * tm_row)
    n_tiles_max = (2 * t) // tm_ffn + N_EXPERTS
    tile_start = jnp.arange(n_tiles_max, dtype=jnp.int32) * tm_ffn
    tile_expert = jnp.minimum(
        jnp.sum((ends[None, :] <= tile_start[:, None]).astype(jnp.int32), axis=1), N_EXPERTS - 1)
    n_tiles = (ends[N_EXPERTS - 1] // tm_ffn).astype(jnp.int32).reshape(1)

    xs = _dispatch(slots3, x2d, norm_g, n_tiles_max * tm_ffn, tm=tm_row)
    ys = _ffn(tile_expert, n_tiles, xs, norm_g, w_gate_up.astype(BF16), w_down.astype(BF16),
              tm=tm_ffn, dense=False)
    return _combine(slots3, x2d, meta, ys, final_g, tm=tm_row, final_norm=final_norm)


def _tiles(batch, seq):
    full = seq >= 4096
    return dict(
        tm_proj=512 if full else 128,
        tq=512 if full else 128,
        ts=512 if full else 128,
        tm_out=512 if full else 128,
        tm_dense=512 if full else 128,
        tm_route=512 if full else 128,
        tm_row=256 if full else 128,
        tm_ffn=512 if full else 128,
    )


def kernel(x, mem, a_norm_mix, a_w_in, a_lam_q1, a_lam_k1, a_lam_q2, a_lam_k2, a_subln, a_mem_norm, a_w_mem_kv, a_w_out, b_norm_mix, b_w_in, b_lb_logits, b_out_norm, b_mem_norm, b_w_mem_kv, b_w_out, dense_norm, dense_w_gate_up, dense_w_down, moe_norm, moe_router, moe_w_gate_up, moe_w_down, final_norm):
    batch, seq, d = x.shape
    mem_len = mem.shape[1]
    t = batch * seq
    ts = _tiles(batch, seq)
    x2d = x.reshape(t, d)
    mem2d = mem.reshape(batch * mem_len, d)
    kv_a = _memkv(mem2d, a_mem_norm, a_w_mem_kv, mem_len)
    kv_b = _memkv(mem2d, b_mem_norm, b_w_mem_kv, mem_len)
    one_tile = jnp.zeros((t // ts["tm_dense"],), jnp.int32)
    all_tiles = jnp.full((1,), t // ts["tm_dense"], jnp.int32)

    for i in range(DEPTH):
        j = i // N_MIXERS
        if i % N_MIXERS == 0:
            lam_init = 0.8 - 0.6 * float(np.exp(-0.3 * i))
            proj = _normproj(x2d, a_norm_mix[j], a_w_in[j].astype(BF16), tm=ts["tm_proj"],
                             scaled_cols=DA_WIDTH, col_scale=DA_HEAD_DIM ** -0.5 * LOG2E)
            lamv = jnp.stack([a_lam_q1[j], a_lam_k1[j], a_lam_q2[j], a_lam_k2[j]])
            o = _diff_attention(proj, lamv, a_subln[j], batch=batch, seq=seq, tq=ts["tq"], lam_init=lam_init)
            x2d = _outproj(o, proj, 3 * DA_WIDTH // MEM_WIDTH, kv_a, j, x2d, a_w_out[j].astype(BF16),
                           tm=ts["tm_out"], seq=seq, mem_len=mem_len)
        else:
            proj, fz32 = _normproj(x2d, b_norm_mix[j], b_w_in[j].astype(BF16), tm=ts["tm_proj"],
                                   f32_cols=(HG_KEY_WIDTH, 2 * HG_KEY_WIDTH))
            o = _hgrn(proj, fz32, b_lb_logits, b_out_norm[j], batch=batch, seq=seq, ts=ts["ts"], layer=j)
            x2d = _outproj(o, proj, (2 * HG_KEY_WIDTH + 2 * HG_VAL_WIDTH) // MEM_WIDTH, kv_b, j, x2d,
                           b_w_out[j].astype(BF16), tm=ts["tm_out"], seq=seq, mem_len=mem_len)
        if i % 2 == 0:
            x2d = _ffn(one_tile, all_tiles, x2d, dense_norm[j], dense_w_gate_up[j].astype(BF16)[None],
                       dense_w_down[j].astype(BF16)[None], tm=ts["tm_dense"], dense=True)
        else:
            x2d = _moe(x2d, moe_norm[j], moe_router[j], moe_w_gate_up[j], moe_w_down[j], final_norm,
                       tm_route=ts["tm_route"], tm_row=ts["tm_row"], tm_ffn=ts["tm_ffn"],
                       final_norm=(i == DEPTH - 1))
    return x2d.reshape(batch, seq, d)
```

```python
import functools

import numpy as np
import jax
import jax.numpy as jnp
from jax import lax
from jax.experimental import pallas as pl
from jax.experimental.pallas import tpu as pltpu

F32 = jnp.float32
BF16 = jnp.bfloat16

DEPTH = 4
N_MIXERS = 2
CHUNK = 64
DA_HEADS = 6
DA_HEAD_DIM = 64
DA_WIDTH = DA_HEADS * 2 * DA_HEAD_DIM
HG_HEADS = 6
HG_KEY_DIM = 128
HG_VAL_DIM = 128
HG_KEY_WIDTH = HG_HEADS * HG_KEY_DIM
HG_VAL_WIDTH = HG_HEADS * HG_VAL_DIM
MEM_HEADS = 4
MEM_HEAD_DIM = 64
MEM_WIDTH = MEM_HEADS * MEM_HEAD_DIM
N_EXPERTS = 8
EPS = 1e-6
MASK_VALUE = -1e30
LOG2E = float(np.log2(np.e))

LANES = 128
SUB_BLOCK = 16
HG_SAFE_LOG_DECAY = -64.0
VMEM_LIMIT = 56 * 1024 * 1024


def _params(sem):
    return pltpu.CompilerParams(dimension_semantics=sem, vmem_limit_bytes=VMEM_LIMIT)


def _rms(x, g):
    return x * lax.rsqrt(jnp.mean(x * x, axis=-1, keepdims=True) + EPS) * g


def _dot(a, b):
    return jnp.dot(a, b, preferred_element_type=F32)


def _dot_nt(a, b):
    return lax.dot_general(a, b, (((1,), (1,)), ((), ())), preferred_element_type=F32)


def _memkv_kernel(mem_ref, g_ref, w_ref, out_ref):
    y = _rms(mem_ref[...], g_ref[0]).astype(BF16)
    out_ref[0] = _dot(y, w_ref[0].astype(BF16)).astype(BF16)


def _memkv(mem2d, g, w, mem_len):
    n_layers, d, n = w.shape
    rows = mem2d.shape[0]
    return pl.pallas_call(
        _memkv_kernel,
        out_shape=jax.ShapeDtypeStruct((n_layers, rows, n), BF16),
        grid=(n_layers, rows // mem_len),
        in_specs=[
            pl.BlockSpec((mem_len, d), lambda l, b: (b, 0)),
            pl.BlockSpec((1, 1, d), lambda l, b: (l, 0, 0)),
            pl.BlockSpec((1, d, n), lambda l, b: (l, 0, 0)),
        ],
        out_specs=pl.BlockSpec((1, mem_len, n), lambda l, b: (l, b, 0)),
        compiler_params=_params(("arbitrary", "arbitrary")),
        name="memkv",
    )(mem2d, g.reshape(n_layers, 1, d), w)


NP_CHUNK = 256


def _normproj_kernel(x_ref, g_ref, w_ref, out_ref, *f32_refs, scaled_cols, col_scale, f32_cols):
    xn = _rms(x_ref[...], g_ref[...]).astype(BF16)
    n = w_ref.shape[1]
    for c0 in range(0, n, NP_CHUNK):
        y = _dot(xn, w_ref[:, c0:c0 + NP_CHUNK])
        if c0 < scaled_cols:
            y = y * col_scale
        out_ref[:, c0:c0 + NP_CHUNK] = y.astype(BF16)
        if f32_cols is not None and f32_cols[0] <= c0 < f32_cols[1]:
            f32_refs[0][:, c0 - f32_cols[0]:c0 - f32_cols[0] + NP_CHUNK] = y


def _normproj(x2d, g, w_bf16, *, tm, scaled_cols=0, col_scale=1.0, f32_cols=None):
    t, d = x2d.shape
    n = w_bf16.shape[1]
    assert n % NP_CHUNK == 0 and scaled_cols % NP_CHUNK == 0
    out_shape = [jax.ShapeDtypeStruct((t, n), BF16)]
    out_specs = [pl.BlockSpec((tm, n), lambda i: (i, 0))]
    if f32_cols is not None:
        assert f32_cols[0] % NP_CHUNK == 0 and f32_cols[1] % NP_CHUNK == 0
        wf = f32_cols[1] - f32_cols[0]
        out_shape.append(jax.ShapeDtypeStruct((t, wf), F32))
        out_specs.append(pl.BlockSpec((tm, wf), lambda i: (i, 0)))
    outs = pl.pallas_call(
        functools.partial(_normproj_kernel, scaled_cols=scaled_cols, col_scale=col_scale, f32_cols=f32_cols),
        out_shape=out_shape,
        grid=(t // tm,),
        in_specs=[
            pl.BlockSpec((tm, d), lambda i: (i, 0)),
            pl.BlockSpec((1, d), lambda i: (0, 0)),
            pl.BlockSpec((d, n), lambda i: (0, 0)),
        ],
        out_specs=out_specs,
        compiler_params=_params(("parallel",)),
        name="normproj",
    )(x2d, g.reshape(1, d), w_bf16)
    return outs if f32_cols is not None else outs[0]


ATT_ROWS = 32


def _fold8(x, op):
    out = x[0:8]
    for r in range(8, x.shape[0], 8):
        out = op(out, x[r:r + 8])
    return out


def _attn_kernel(dec_ref, q_ref, k_ref, v_ref, bias_ref, lamv_ref, g_ref, o_ref,
                 vt_ref, s0_ref, s1_ref, mx0_ref, mx1_ref, p_ref, st_ref, acc_ref, *, tq, lam_init):
    h = pl.program_id(1)
    nq = q_ref.shape[0] // tq
    n_pairs = nq * (nq + 1) // 2
    assert n_pairs % 2 == 0
    s_refs, mx_refs = (s0_ref, s1_ref), (mx0_ref, mx1_ref)

    for j in range(nq):
        vt_ref[j] = v_ref[j * tq:(j + 1) * tq, :].astype(F32).T.astype(BF16)
    for ref in (s1_ref, mx1_ref, acc_ref, st_ref):
        ref[...] = jnp.zeros_like(ref)

    dec = dec_ref[h]
    lo_map = lax.broadcasted_iota(jnp.int32, (tq, LANES), 1) < DA_HEAD_DIM
    lv = lamv_ref[...]
    lam = (jnp.exp(jnp.sum(lv[0:1] * lv[1:2], axis=-1, keepdims=True))
           - jnp.exp(jnp.sum(lv[2:3] * lv[3:4], axis=-1, keepdims=True)) + lam_init)
    out_gain = g_ref[...] * (1.0 - lam_init)

    def step(slot, pair_a, pair_b):
        (qa, ka), (qb, kb) = pair_a, pair_b
        s_a, s_b = s_refs[slot], s_refs[1 - slot]

        q = q_ref[pl.ds(pl.multiple_of(qa * tq, tq), tq), :]
        kt = k_ref[pl.ds(pl.multiple_of(ka * tq, tq), tq), :]
        zero = jnp.zeros_like(q)
        bias = bias_ref[0, (ka == qa).astype(jnp.int32)]
        for mi, qz in enumerate((jnp.where(lo_map, q, zero), jnp.where(lo_map, zero, q))):
            s = _dot_nt(kt, qz) + bias
            s_a[mi] = s
            mx_refs[slot][mi] = _fold8(s, jnp.maximum)

        first = kb == 0
        vt = vt_ref[kb]
        for mi in range(2):
            m_old = jnp.where(first, MASK_VALUE, st_ref[mi, 0:1, :])
            l_old = jnp.where(first, 0.0, st_ref[mi, 1:2, :])
            mp = m_old - dec
            mn = jnp.maximum(mp, jnp.max(mx_refs[1 - slot][mi], axis=0, keepdims=True))
            alpha = jnp.exp2(mp - mn)
            l8 = None
            for c in range(0, tq, ATT_ROWS):
                p = jnp.exp2(s_b[mi, c:c + ATT_ROWS, :] - mn)
                p_ref[mi, c:c + ATT_ROWS, :] = p.astype(BF16)
                cs = _fold8(p, jnp.add)
                l8 = cs if l8 is None else l8 + cs
            st_ref[mi, 0:1, :] = mn
            st_ref[mi, 1:2, :] = alpha * l_old + jnp.sum(l8, axis=0, keepdims=True)
            kept = jnp.where(first, 0.0, alpha * acc_ref[mi])
            acc_ref[mi] = kept + _dot(vt, p_ref[mi])

        @pl.when(kb == qb)
        def _():
            ot = acc_ref[0] / st_ref[0, 1:2, :] - lam * (acc_ref[1] / st_ref[1, 1:2, :])
            ot = ot * lax.rsqrt(jnp.mean(ot * ot, axis=0, keepdims=True) + EPS) * out_gain
            o_ref[pl.ds(pl.multiple_of(qb * tq, tq), tq), :] = ot.T.astype(BF16)

    def advance(pair):
        qx, kx = pair
        row_done = kx == qx
        last = jnp.logical_and(row_done, qx == nq - 1)
        return (jnp.where(jnp.logical_and(row_done, jnp.logical_not(last)), qx + 1, qx),
                jnp.where(last, kx, jnp.where(row_done, 0, kx + 1)))

    def body(_, pairs):
        pa, pb = pairs
        step(0, pa, pb)
        pa2 = advance(pa)
        step(1, pa2, pa)
        return (advance(pa2), pa2)

    origin = (jnp.int32(0), jnp.int32(0))
    pa, pb = lax.fori_loop(0, n_pairs // 2, body, (origin, origin))
    step(0, pa, pb)


def _alibi_tiles(tq):
    slopes = 2.0 ** (-8.0 * np.arange(1, DA_HEADS + 1) / DA_HEADS)
    kpos = np.arange(tq)[:, None]
    qpos = np.arange(tq)[None, :]
    low = slopes[:, None, None] * LOG2E * (kpos - qpos)[None]
    diag = np.where((kpos // CHUNK <= qpos // CHUNK)[None],
                    -slopes[:, None, None] * LOG2E * np.abs(qpos - kpos)[None], MASK_VALUE)
    dec = slopes * LOG2E * tq
    return jnp.asarray(dec, F32), jnp.asarray(np.stack([low, diag], axis=1), F32)


def _diff_attention(proj, lamv, subln_g, *, batch, seq, tq, lam_init):
    t = proj.shape[0]
    nq = seq // tq
    dec, bias = _alibi_tiles(tq)
    grid_spec = pltpu.PrefetchScalarGridSpec(
        num_scalar_prefetch=1,
        grid=(batch, DA_HEADS),
        in_specs=[
            pl.BlockSpec((seq, LANES), lambda b, h, dec: (b, h)),
            pl.BlockSpec((seq, LANES), lambda b, h, dec: (b, DA_HEADS + h)),
            pl.BlockSpec((seq, LANES), lambda b, h, dec: (b, 2 * DA_HEADS + h)),
            pl.BlockSpec((1, 2, tq, tq), lambda b, h, dec: (h, 0, 0, 0)),
            pl.BlockSpec((4, DA_HEAD_DIM), lambda b, h, dec: (0, 0)),
            pl.BlockSpec((LANES, 1), lambda b, h, dec: (0, 0)),
        ],
        out_specs=pl.BlockSpec((seq, LANES), lambda b, h, dec: (b, h)),
        scratch_shapes=[
            pltpu.VMEM((nq, LANES, tq), BF16),
            pltpu.VMEM((2, tq, tq), F32),
            pltpu.VMEM((2, tq, tq), F32),
            pltpu.VMEM((2, 8, tq), F32),
            pltpu.VMEM((2, 8, tq), F32),
            pltpu.VMEM((2, tq, tq), BF16),
            pltpu.VMEM((2, 8, tq), F32),
            pltpu.VMEM((2, LANES, tq), F32),
        ],
    )
    return pl.pallas_call(
        functools.partial(_attn_kernel, tq=tq, lam_init=lam_init),
        out_shape=jax.ShapeDtypeStruct((t, DA_WIDTH), BF16),
        grid_spec=grid_spec,
        compiler_params=_params(("parallel", "parallel")),
        name="diff_attn",
    )(dec, proj, proj, proj, bias, lamv, subln_g.reshape(LANES, 1))


def _hgrn_kernel(q_ref, fz_ref, v_ref, gate_ref, lbl_ref, g_ref, o_ref, st_ref, *, layer, n_chunks):
    @pl.when(pl.program_id(2) == 0)
    def _():
        st_ref[...] = jnp.zeros_like(st_ref)

    lbl = lbl_ref[...]
    e = jnp.exp(lbl - jnp.max(lbl, axis=0, keepdims=True))
    pr = e / jnp.sum(e, axis=0, keepdims=True)
    lb = jnp.sum(pr[0:layer + 1], axis=0, keepdims=True) - pr[0:1]

    rr = lax.broadcasted_iota(jnp.int32, (CHUNK, CHUNK), 0)
    cc = lax.broadcasted_iota(jnp.int32, (CHUNK, CHUNK), 1)
    tril = (cc <= rr).astype(F32)
    sub_row = lax.broadcasted_iota(jnp.int32, (SUB_BLOCK, LANES), 0)
    n_sub = CHUNK // SUB_BLOCK
    gain = g_ref[...]

    z_all = fz_ref[...]
    f_all = lb + (1.0 - lb) * (1.0 / (1.0 + jnp.exp(-z_all)))
    k_all = (1.0 - lb) * (1.0 / (1.0 + jnp.exp(z_all)))
    logf = jnp.log(f_all)
    lf_hi = logf.astype(BF16)
    lf_r1 = logf - lf_hi.astype(F32)
    lf_mid = lf_r1.astype(BF16)
    lf_lo = (lf_r1 - lf_mid.astype(F32)).astype(BF16)
    lf3 = jnp.concatenate([lf_hi, lf_mid, lf_lo], axis=1)
    r2 = lax.broadcasted_iota(jnp.int32, (2 * CHUNK, CHUNK), 0)
    c2 = lax.broadcasted_iota(jnp.int32, (2 * CHUNK, CHUNK), 1)
    in_chunk = jnp.logical_and(r2 < CHUNK, c2 <= r2)
    in_sub = jnp.logical_and(jnp.logical_and(r2 >= CHUNK, c2 <= r2 - CHUNK),
                             c2 // SUB_BLOCK == (r2 - CHUNK) // SUB_BLOCK)
    cum_w = jnp.where(jnp.logical_or(in_chunk, in_sub), 1.0, 0.0).astype(BF16)
    cums = []
    for c in range(n_chunks):
        e3 = _dot(cum_w, lf3[c * CHUNK:(c + 1) * CHUNK])
        cums.append(e3[:, 0:LANES] + e3[:, LANES:2 * LANES] + e3[:, 2 * LANES:3 * LANES])
    min_rel = cums[0][CHUNK:]
    for c in range(1, n_chunks):
        min_rel = jnp.minimum(min_rel, cums[c][CHUNK:])
    safe = jnp.min(min_rel) >= HG_SAFE_LOG_DECAY

    blk_r, blk_c = rr // SUB_BLOCK, cc // SUB_BLOCK
    mask_sub = jnp.logical_and(blk_r == blk_c, cc <= rr)
    mask_16 = jnp.logical_and(blk_r % 2 == 1, blk_c == blk_r - 1)
    zeros16 = jnp.zeros((SUB_BLOCK, LANES), F32)
    zeros32 = jnp.zeros((2 * SUB_BLOCK, LANES), F32)

    @pl.when(safe)
    def _():
        st = st_ref[...]
        for c in range(n_chunks):
            rows = slice(c * CHUNK, (c + 1) * CHUNK)
            q = q_ref[rows, :].astype(F32)
            vb = v_ref[rows, :]
            k = k_all[rows]
            b, rel = cums[c][:CHUNK], cums[c][CHUNK:]
            b15, b31, b47, b63 = b[15:16], b[31:32], b[47:48], b[63:64]
            q32 = jnp.concatenate([zeros32, q[32:64] * jnp.exp(b[32:64] - b31)], axis=0)
            k32 = jnp.concatenate([k[0:32] * jnp.exp(b31 - b[0:32]), zeros32], axis=0)
            q16 = jnp.concatenate([zeros16, q[16:32] * jnp.exp(b[16:32] - b15),
                                   zeros16, q[48:64] * jnp.exp(b[48:64] - b47)], axis=0)
            k16 = jnp.concatenate([k[0:16] * jnp.exp(b15 - b[0:16]), zeros16,
                                   k[32:48] * jnp.exp(b47 - b[32:48]), zeros16], axis=0)
            attn = (_dot_nt(q32.astype(BF16), k32.astype(BF16))
                    + jnp.where(mask_16, _dot_nt(q16.astype(BF16), k16.astype(BF16)), 0.0)
                    + jnp.where(mask_sub, _dot_nt((q * jnp.exp(rel)).astype(BF16),
                                                  (k * jnp.exp(-rel)).astype(BF16)), 0.0))
            o = _dot(attn.astype(BF16), vb) + _dot_nt((q * jnp.exp(b)).astype(BF16), st.astype(BF16))
            upd = lax.dot_general(vb, (k * jnp.exp(b63 - b)).astype(BF16), (((0,), (0,)), ((), ())),
                                  preferred_element_type=F32)
            st = st * jnp.exp(b63) + upd
            gt = gate_ref[rows, :].astype(F32)
            o_ref[rows, :] = (_rms(o, gain) * (gt * (1.0 / (1.0 + jnp.exp(-gt))))).astype(BF16)
        st_ref[...] = st

    def chunk(c, _):
        rows = pl.ds(pl.multiple_of(c * CHUNK, CHUNK), CHUNK)
        z = fz_ref[rows, :]
        q = q_ref[rows, :].astype(F32)
        v = v_ref[rows, :].astype(F32)
        f = lb + (1.0 - lb) * (1.0 / (1.0 + jnp.exp(-z)))
        k = (1.0 - lb) * (1.0 / (1.0 + jnp.exp(z)))
        b = jnp.dot(tril, jnp.log(f), preferred_element_type=F32, precision=lax.Precision.HIGHEST)
        vb = v.astype(BF16)
        st = st_ref[...]
        o_inter = _dot_nt((q * jnp.exp(b)).astype(BF16), st.astype(BF16))

        outs = []
        for i in range(n_sub):
            r0 = i * SUB_BLOCK
            qi = q[r0:r0 + SUB_BLOCK]
            ki = k[r0:r0 + SUB_BLOCK]
            bi = b[r0:r0 + SUB_BLOCK]
            vi = v[r0:r0 + SUB_BLOCK]
            oi = jnp.zeros((SUB_BLOCK, LANES), F32)
            for s in range(SUB_BLOCK):
                dcy = jnp.where(sub_row >= s, jnp.exp(bi - bi[s:s + 1]), 0.0)
                a_s = jnp.sum(dcy * qi * ki[s:s + 1], axis=-1, keepdims=True)
                oi = oi + a_s * vi[s:s + 1]
            if i > 0:
                ref_b = b[r0 - 1:r0]
                qd = (qi * jnp.exp(bi - ref_b)).astype(BF16)
                kd = (k[0:r0] * jnp.exp(ref_b - b[0:r0])).astype(BF16)
                oi = oi + _dot(_dot_nt(qd, kd).astype(BF16), vb[0:r0])
            outs.append(oi)
        o = jnp.concatenate(outs, axis=0) + o_inter

        b_last = b[CHUNK - 1:CHUNK]
        kd = (k * jnp.exp(b_last - b)).astype(BF16)
        upd = lax.dot_general(vb, kd, (((0,), (0,)), ((), ())), preferred_element_type=F32)
        st_ref[...] = st * jnp.exp(b_last) + upd

        gt = gate_ref[rows, :].astype(F32)
        o_ref[rows, :] = (_rms(o, gain) * (gt * (1.0 / (1.0 + jnp.exp(-gt))))).astype(BF16)
        return 0

    @pl.when(jnp.logical_not(safe))
    def _():
        lax.fori_loop(0, n_chunks, chunk, 0)


def _hgrn(proj, fz32, lb_logits, out_norm_g, *, batch, seq, ts, layer):
    t = proj.shape[0]
    ns = seq // ts
    n_b = lb_logits.shape[0]
    return pl.pallas_call(
        functools.partial(_hgrn_kernel, layer=layer, n_chunks=ts // CHUNK),
        out_shape=jax.ShapeDtypeStruct((t, HG_VAL_WIDTH), BF16),
        grid=(batch, HG_HEADS, ns),
        in_specs=[
            pl.BlockSpec((ts, LANES), lambda b, h, i: (b * ns + i, h)),
            pl.BlockSpec((ts, LANES), lambda b, h, i: (b * ns + i, h)),
            pl.BlockSpec((ts, LANES), lambda b, h, i: (b * ns + i, 2 * HG_HEADS + h)),
            pl.BlockSpec((ts, LANES), lambda b, h, i: (b * ns + i, 3 * HG_HEADS + h)),
            pl.BlockSpec((n_b, LANES), lambda b, h, i: (0, h)),
            pl.BlockSpec((1, LANES), lambda b, h, i: (0, 0)),
        ],
        out_specs=pl.BlockSpec((ts, LANES), lambda b, h, i: (b * ns + i, h)),
        scratch_shapes=[pltpu.VMEM((HG_VAL_DIM, HG_KEY_DIM), F32)],
        compiler_params=_params(("parallel", "parallel", "arbitrary")),
        name="hgrn2",
    )(proj, fz32, proj, proj, lb_logits, out_norm_g.reshape(1, LANES))


def _outproj_kernel(o_ref, qm_ref, kv_ref, x_ref, w_ref, out_ref):
    qm = qm_ref[...]
    kmem = kv_ref[0, :, 0:MEM_WIDTH]
    vmem = kv_ref[0, :, MEM_WIDTH:2 * MEM_WIDTH]
    head = lax.broadcasted_iota(jnp.int32, qm.shape, 1) // MEM_HEAD_DIM
    om = jnp.zeros(qm.shape, F32)
    for hh in range(MEM_HEADS):
        qh = jnp.where(head == hh, qm, jnp.zeros_like(qm))
        s = _dot_nt(qh, kmem) * (MEM_HEAD_DIM ** -0.5)
        p = jnp.exp(s - jnp.max(s, axis=-1, keepdims=True))
        p = p / jnp.sum(p, axis=-1, keepdims=True)
        om = jnp.where(head == hh, _dot(p.astype(BF16), vmem), om)
    wo = o_ref.shape[1]
    out_ref[...] = (x_ref[...] + _dot(o_ref[...], w_ref[0:wo, :])
                    + _dot(om.astype(BF16), w_ref[wo:wo + MEM_WIDTH, :]))


def _outproj(o, proj, qm_block, memkv, layer, x2d, w_out_bf16, *, tm, seq, mem_len):
    t, d = x2d.shape
    wo = o.shape[1]
    tiles_per_batch = seq // tm
    return pl.pallas_call(
        _outproj_kernel,
        out_shape=jax.ShapeDtypeStruct((t, d), F32),
        grid=(t // tm,),
        in_specs=[
            pl.BlockSpec((tm, wo), lambda i: (i, 0)),
            pl.BlockSpec((tm, MEM_WIDTH), lambda i: (i, qm_block)),
            pl.BlockSpec((1, mem_len, 2 * MEM_WIDTH), lambda i: (layer, i // tiles_per_batch, 0)),
            pl.BlockSpec((tm, d), lambda i: (i, 0)),
            pl.BlockSpec((wo + MEM_WIDTH, d), lambda i: (0, 0)),
        ],
        out_specs=pl.BlockSpec((tm, d), lambda i: (i, 0)),
        compiler_params=_params(("parallel",)),
        name="outproj",
    )(o, proj, memkv, x2d, w_out_bf16)


MXU_COLS = 256


def _ffn_kernel(te_ref, nt_ref, x_ref, g_ref, wg_ref, wu_ref, wd_ref, out_ref, *, dense):
    del te_ref
    i = pl.program_id(0)
    f = wd_ref.shape[1]
    half = f // 2 // MXU_COLS * MXU_COLS

    @pl.when(i < nt_ref[0])
    def _():
        x = x_ref[...]
        xn = (_rms(x, g_ref[...]) if dense else x).astype(BF16)
        y = x if dense else None
        for c0, c1 in ((0, half), (half, f)):
            gt = _dot(xn, wg_ref[0, :, c0:c1])
            up = _dot(xn, wu_ref[0, :, c0:c1])
            act = (gt * (1.0 / (1.0 + jnp.exp(-gt))) * up).astype(BF16)
            part = _dot(act, wd_ref[0, c0:c1, :])
            y = part if y is None else y + part
        out_ref[...] = y

    @pl.when(i >= nt_ref[0])
    def _():
        out_ref[...] = jnp.zeros_like(out_ref)


def _ffn(tile_expert, n_tiles, x2d, g, w_gate_up_bf16, w_down_bf16, *, tm, dense):
    r, d = x2d.shape
    f = w_down_bf16.shape[1]
    resident = pl.Buffered(1)
    grid_spec = pltpu.PrefetchScalarGridSpec(
        num_scalar_prefetch=2,
        grid=(r // tm,),
        in_specs=[
            pl.BlockSpec((tm, d), lambda i, te, nt: (jnp.minimum(i, nt[0] - 1), 0)),
            pl.BlockSpec((1, d), lambda i, te, nt: (0, 0)),
            pl.BlockSpec((1, d, f), lambda i, te, nt: (te[i], 0, 0), pipeline_mode=resident),
            pl.BlockSpec((1, d, f), lambda i, te, nt: (te[i], 0, 1), pipeline_mode=resident),
            pl.BlockSpec((1, f, d), lambda i, te, nt: (te[i], 0, 0), pipeline_mode=resident),
        ],
        out_specs=pl.BlockSpec((tm, d), lambda i, te, nt: (i, 0)),
    )
    return pl.pallas_call(
        functools.partial(_ffn_kernel, dense=dense),
        out_shape=jax.ShapeDtypeStruct((r, d), F32),
        grid_spec=grid_spec,
        compiler_params=_params(("parallel",)),
        name="swiglu_dense" if dense else "swiglu_grouped",
    )(tile_expert, n_tiles, x2d, g.reshape(1, d), w_gate_up_bf16, w_gate_up_bf16, w_down_bf16)


META_E1, META_E2, META_W1, META_W2, META_R1, META_R2 = range(6)


def _router_kernel(x_ref, g_ref, wr_ref, meta_ref, cnt_ref, run_ref):
    @pl.when(pl.program_id(0) == 0)
    def _():
        run_ref[...] = jnp.zeros_like(run_ref)

    tm = x_ref.shape[0]
    xn = _rms(x_ref[...], g_ref[...])
    x_hi = xn.astype(BF16)
    x_lo = (xn - x_hi.astype(F32)).astype(BF16)
    w = wr_ref[...]
    w_hi = w.astype(BF16)
    w_lo = (w - w_hi.astype(F32)).astype(BF16)
    logits = _dot(x_hi, w_hi) + (_dot(x_lo, w_hi) + _dot(x_hi, w_lo))
    lane = lax.broadcasted_iota(jnp.int32, logits.shape, 1)
    logits = jnp.where(lane < N_EXPERTS, logits, -jnp.inf)

    def top(vals):
        best = jnp.max(vals, axis=-1, keepdims=True)
        idx = jnp.min(jnp.where(vals == best, lane, LANES), axis=-1, keepdims=True)
        return best, idx

    v1, e1 = top(logits)
    v2, e2 = top(jnp.where(lane == e1, -jnp.inf, logits))
    ex = jnp.exp(v2 - v1)
    w1 = 1.0 / (1.0 + ex)
    w2 = ex / (1.0 + ex)

    oh1 = (lane == e1).astype(F32)
    oh2 = (lane == e2).astype(F32)
    rr = lax.broadcasted_iota(jnp.int32, (tm, tm), 0)
    cc = lax.broadcasted_iota(jnp.int32, (tm, tm), 1)
    strict = (cc < rr).astype(BF16)
    before = _dot(strict, (oh1 + oh2).astype(BF16)) + run_ref[0:1, :]
    r1 = jnp.sum(before * oh1, axis=-1, keepdims=True)
    r2 = jnp.sum(before * oh2, axis=-1, keepdims=True)
    run_ref[0:1, :] = run_ref[0:1, :] + jnp.sum(oh1 + oh2, axis=0, keepdims=True)

    meta = jnp.zeros(logits.shape, F32)
    for slot, val in ((META_E1, e1.astype(F32)), (META_E2, e2.astype(F32)), (META_W1, w1), (META_W2, w2),
                      (META_R1, r1), (META_R2, r2)):
        meta = jnp.where(lane == slot, val, meta)
    meta_ref[...] = meta
    cnt_ref[...] = run_ref[...]


def _router(x2d, g, w_router_padded, *, tm):
    t, d = x2d.shape
    return pl.pallas_call(
        _router_kernel,
        out_shape=[jax.ShapeDtypeStruct((t, LANES), F32), jax.ShapeDtypeStruct((8, LANES), F32)],
        grid=(t // tm,),
        in_specs=[
            pl.BlockSpec((tm, d), lambda i: (i, 0)),
            pl.BlockSpec((1, d), lambda i: (0, 0)),
            pl.BlockSpec((d, LANES), lambda i: (0, 0)),
        ],
        out_specs=[pl.BlockSpec((tm, LANES), lambda i: (i, 0)), pl.BlockSpec((8, LANES), lambda i: (0, 0))],
        scratch_shapes=[pltpu.VMEM((8, LANES), F32)],
        compiler_params=_params(("arbitrary",)),
        name="moe_router",
    )(x2d, g.reshape(1, d), w_router_padded)


ROW_DMA_UNROLL = 8


def _dispatch_kernel(slot_ref, pad_ref, x_ref, g_ref, xs_ref, xn_ref, zrow_ref, sem):
    tm = x_ref.shape[0]

    @pl.when(pl.program_id(0) == 0)
    def _():
        zrow_ref[...] = jnp.zeros_like(zrow_ref)

        def zero_copy(r):
            return pltpu.make_async_copy(zrow_ref.at[pl.ds(0, 1)], xs_ref.at[pl.ds(r, 1)], sem)

        def zero_block(r8):
            rows = pl.ds(pl.multiple_of(r8 * 8, 8), 8)
            return pltpu.make_async_copy(zrow_ref, xs_ref.at[rows], sem)

        tail = (pad_ref[0, 2 * N_EXPERTS - 1] // 8, xs_ref.shape[0] // 8)
        for e in range(N_EXPERTS):
            lo, hi = pad_ref[0, e], pad_ref[0, N_EXPERTS + e]
            lax.fori_loop(lo, hi, lambda r, c: (zero_copy(r).start(), c)[1], 0)
        lax.fori_loop(tail[0], tail[1], lambda r8, c: (zero_block(r8).start(), c)[1], 0)
        for e in range(N_EXPERTS):
            lo, hi = pad_ref[0, e], pad_ref[0, N_EXPERTS + e]
            lax.fori_loop(lo, hi, lambda r, c: (zero_copy(r).wait(), c)[1], 0)
        lax.fori_loop(tail[0], tail[1], lambda r8, c: (zero_block(r8).wait(), c)[1], 0)

    xn_ref[...] = _rms(x_ref[...], g_ref[...])

    def row_copy(r, k):
        return pltpu.make_async_copy(xn_ref.at[pl.ds(r, 1)], xs_ref.at[pl.ds(slot_ref[0, 0, 2 * r + k], 1)], sem)

    def start(r, _):
        row_copy(r, 0).start()
        row_copy(r, 1).start()
        return 0

    def wait(r, _):
        row_copy(r, 0).wait()
        row_copy(r, 1).wait()
        return 0

    lax.fori_loop(0, tm, start, 0, unroll=ROW_DMA_UNROLL)
    lax.fori_loop(0, tm, wait, 0, unroll=ROW_DMA_UNROLL)


def _dispatch(slots3, pad_rows, x2d, g, n_slots, *, tm):
    t, d = x2d.shape
    return pl.pallas_call(
        _dispatch_kernel,
        out_shape=jax.ShapeDtypeStruct((n_slots, d), F32),
        grid=(t // tm,),
        in_specs=[
            pl.BlockSpec((1, 1, 2 * tm), lambda i: (i, 0, 0), memory_space=pltpu.SMEM),
            pl.BlockSpec((1, 2 * N_EXPERTS), lambda i: (0, 0), memory_space=pltpu.SMEM),
            pl.BlockSpec((tm, d), lambda i: (i, 0)),
            pl.BlockSpec((1, d), lambda i: (0, 0)),
        ],
        out_specs=pl.BlockSpec(memory_space=pl.ANY),
        scratch_shapes=[pltpu.VMEM((tm, d), F32), pltpu.VMEM((8, d), F32), pltpu.SemaphoreType.DMA],
        compiler_params=_params(("arbitrary",)),
        name="moe_dispatch",
    )(slots3, pad_rows, x2d, g.reshape(1, d))


def _combine_kernel(slot_ref, x_ref, meta_ref, y_ref, g_ref, out_ref, ybuf_ref, sem, *, final_norm):
    tm = x_ref.shape[0]

    def row_copy(r, k):
        return pltpu.make_async_copy(y_ref.at[pl.ds(slot_ref[0, 0, 2 * r + k], 1)], ybuf_ref.at[k, pl.ds(r, 1)], sem)

    def start(r, _):
        row_copy(r, 0).start()
        row_copy(r, 1).start()
        return 0

    def wait(r, _):
        row_copy(r, 0).wait()
        row_copy(r, 1).wait()
        return 0

    lax.fori_loop(0, tm, start, 0, unroll=ROW_DMA_UNROLL)
    lax.fori_loop(0, tm, wait, 0, unroll=ROW_DMA_UNROLL)
    meta = meta_ref[...]
    w1 = meta[:, META_W1:META_W1 + 1]
    w2 = meta[:, META_W2:META_W2 + 1]
    out = x_ref[...] + (w1 * ybuf_ref[0] + w2 * ybuf_ref[1])
    out_ref[...] = _rms(out, g_ref[...]) if final_norm else out


def _combine(slots3, x2d, meta, y, g, *, tm, final_norm):
    t, d = x2d.shape
    return pl.pallas_call(
        functools.partial(_combine_kernel, final_norm=final_norm),
        out_shape=jax.ShapeDtypeStruct((t, d), F32),
        grid=(t // tm,),
        in_specs=[
            pl.BlockSpec((1, 1, 2 * tm), lambda i: (i, 0, 0), memory_space=pltpu.SMEM),
            pl.BlockSpec((tm, d), lambda i: (i, 0)),
            pl.BlockSpec((tm, LANES), lambda i: (i, 0)),
            pl.BlockSpec(memory_space=pl.ANY),
            pl.BlockSpec((1, d), lambda i: (0, 0)),
        ],
        out_specs=pl.BlockSpec((tm, d), lambda i: (i, 0)),
        scratch_shapes=[pltpu.VMEM((2, tm, d), F32), pltpu.SemaphoreType.DMA],
        compiler_params=_params(("arbitrary",)),
        name="moe_combine",
    )(slots3, x2d, meta, y, g.reshape(1, d))


def _moe(x2d, norm_g, w_router, w_gate_up_bf16, w_down_bf16, first_expert, final_g,
         *, tm_route, tm_row, tm_ffn, final_norm):
    t, d = x2d.shape
    wr = jnp.zeros((d, LANES), F32).at[:, :N_EXPERTS].set(w_router)
    meta, counts = _router(x2d, norm_g, wr, tm=tm_route)

    cnt = counts[0, :N_EXPERTS].astype(jnp.int32)
    padded = (cnt + tm_ffn - 1) // tm_ffn * tm_ffn
    ends = jnp.cumsum(padded)
    starts = ends - padded
    e12 = meta[:, META_E1:META_E2 + 1].astype(jnp.int32)
    r12 = meta[:, META_R1:META_R2 + 1].astype(jnp.int32)
    slots = starts[e12] + r12
    slots3 = slots.reshape(t // tm_row, 1, 2 * tm_row)
    n_tiles_max = (2 * t) // tm_ffn + N_EXPERTS
    tile_start = jnp.arange(n_tiles_max, dtype=jnp.int32) * tm_ffn
    tile_expert = jnp.minimum(
        jnp.sum((ends[None, :] <= tile_start[:, None]).astype(jnp.int32), axis=1), N_EXPERTS - 1)
    n_tiles = (ends[N_EXPERTS - 1] // tm_ffn).astype(jnp.int32).reshape(1)

    pad_rows = jnp.concatenate([starts + cnt, ends]).astype(jnp.int32).reshape(1, 2 * N_EXPERTS)
    xs = _dispatch(slots3, pad_rows, x2d, norm_g, n_tiles_max * tm_ffn, tm=tm_row)
    ys = _ffn(tile_expert + first_expert, n_tiles, xs, norm_g, w_gate_up_bf16, w_down_bf16,
              tm=tm_ffn, dense=False)
    return _combine(slots3, x2d, meta, ys, final_g, tm=tm_row, final_norm=final_norm)


def _tiles(batch, seq):
    full = seq >= 4096
    return dict(
        tm_proj=512 if full else 128,
        tq=512 if full else 128,
        ts=512 if full else 128,
        tm_out=512 if full else 128,
        tm_dense=512 if full else 128,
        tm_route=512 if full else 128,
        tm_row=256 if full else 128,
        tm_ffn=512 if full else 128,
    )


def kernel(x, mem, a_norm_mix, a_w_in, a_lam_q1, a_lam_k1, a_lam_q2, a_lam_k2, a_subln, a_mem_norm, a_w_mem_kv, a_w_out, b_norm_mix, b_w_in, b_lb_logits, b_out_norm, b_mem_norm, b_w_mem_kv, b_w_out, dense_norm, dense_w_gate_up, dense_w_down, moe_norm, moe_router, moe_w_gate_up, moe_w_down, final_norm):
    batch, seq, d = x.shape
    mem_len = mem.shape[1]
    t = batch * seq
    ts = _tiles(batch, seq)
    x2d = x.reshape(t, d)
    mem2d = mem.reshape(batch * mem_len, d)
    kv_a = _memkv(mem2d, a_mem_norm, a_w_mem_kv, mem_len)
    kv_b = _memkv(mem2d, b_mem_norm, b_w_mem_kv, mem_len)
    all_tiles = jnp.full((1,), t // ts["tm_dense"], jnp.int32)
    dense_gu, dense_dn = dense_w_gate_up.astype(BF16), dense_w_down.astype(BF16)
    moe_gu = moe_w_gate_up.astype(BF16).reshape((-1,) + moe_w_gate_up.shape[2:])
    moe_dn = moe_w_down.astype(BF16).reshape((-1,) + moe_w_down.shape[2:])

    for i in range(DEPTH):
        j = i // N_MIXERS
        if i % N_MIXERS == 0:
            lam_init = 0.8 - 0.6 * float(np.exp(-0.3 * i))
            proj = _normproj(x2d, a_norm_mix[j], a_w_in[j].astype(BF16), tm=ts["tm_proj"],
                             scaled_cols=DA_WIDTH, col_scale=DA_HEAD_DIM ** -0.5 * LOG2E)
            lamv = jnp.stack([a_lam_q1[j], a_lam_k1[j], a_lam_q2[j], a_lam_k2[j]])
            o = _diff_attention(proj, lamv, a_subln[j], batch=batch, seq=seq, tq=ts["tq"], lam_init=lam_init)
            x2d = _outproj(o, proj, 3 * DA_WIDTH // MEM_WIDTH, kv_a, j, x2d, a_w_out[j].astype(BF16),
                           tm=ts["tm_out"], seq=seq, mem_len=mem_len)
        else:
            proj, fz32 = _normproj(x2d, b_norm_mix[j], b_w_in[j].astype(BF16), tm=ts["tm_proj"],
                                   f32_cols=(HG_KEY_WIDTH, 2 * HG_KEY_WIDTH))
            o = _hgrn(proj, fz32, b_lb_logits, b_out_norm[j], batch=batch, seq=seq, ts=ts["ts"], layer=j)
            x2d = _outproj(o, proj, (2 * HG_KEY_WIDTH + 2 * HG_VAL_WIDTH) // MEM_WIDTH, kv_b, j, x2d,
                           b_w_out[j].astype(BF16), tm=ts["tm_out"], seq=seq, mem_len=mem_len)
        if i % 2 == 0:
            layer_tiles = jnp.full((t // ts["tm_dense"],), j, jnp.int32)
            x2d = _ffn(layer_tiles, all_tiles, x2d, dense_norm[j], dense_gu, dense_dn, tm=ts["tm_dense"], dense=True)
        else:
            x2d = _moe(x2d, moe_norm[j], moe_router[j], moe_gu, moe_dn, j * N_EXPERTS, final_norm,
                       tm_route=ts["tm_route"], tm_row=ts["tm_row"], tm_ffn=ts["tm_ffn"],
                       final_norm=(i == DEPTH - 1))
    return x2d.reshape(batch, seq, d)
```

```python
import functools

import numpy as np
import jax
import jax.numpy as jnp
from jax import lax
from jax.experimental import pallas as pl
from jax.experimental.pallas import tpu as pltpu

F32 = jnp.float32
BF16 = jnp.bfloat16

DEPTH = 4
N_MIXERS = 2
CHUNK = 64
DA_HEADS = 6
DA_HEAD_DIM = 64
DA_WIDTH = DA_HEADS * 2 * DA_HEAD_DIM
HG_HEADS = 6
HG_KEY_DIM = 128
HG_VAL_DIM = 128
HG_KEY_WIDTH = HG_HEADS * HG_KEY_DIM
HG_VAL_WIDTH = HG_HEADS * HG_VAL_DIM
MEM_HEADS = 4
MEM_HEAD_DIM = 64
MEM_WIDTH = MEM_HEADS * MEM_HEAD_DIM
N_EXPERTS = 8
EPS = 1e-6
MASK_VALUE = -1e30
LOG2E = float(np.log2(np.e))

LANES = 128
SUB_BLOCK = 16
HG_SAFE_LOG_DECAY = -64.0
VMEM_LIMIT = 56 * 1024 * 1024


def _params(sem):
    return pltpu.CompilerParams(dimension_semantics=sem, vmem_limit_bytes=VMEM_LIMIT)


def _rms(x, g):
    return x * lax.rsqrt(jnp.mean(x * x, axis=-1, keepdims=True) + EPS) * g


def _dot(a, b):
    return jnp.dot(a, b, preferred_element_type=F32)


def _dot_nt(a, b):
    return lax.dot_general(a, b, (((1,), (1,)), ((), ())), preferred_element_type=F32)


def _memkv_kernel(mem_ref, g_ref, w_ref, out_ref):
    y = _rms(mem_ref[...], g_ref[0]).astype(BF16)
    out_ref[0] = _dot(y, w_ref[0].astype(BF16)).astype(BF16)


def _memkv(mem2d, g, w, mem_len):
    n_layers, d, n = w.shape
    rows = mem2d.shape[0]
    return pl.pallas_call(
        _memkv_kernel,
        out_shape=jax.ShapeDtypeStruct((n_layers, rows, n), BF16),
        grid=(n_layers, rows // mem_len),
        in_specs=[
            pl.BlockSpec((mem_len, d), lambda l, b: (b, 0)),
            pl.BlockSpec((1, 1, d), lambda l, b: (l, 0, 0)),
            pl.BlockSpec((1, d, n), lambda l, b: (l, 0, 0)),
        ],
        out_specs=pl.BlockSpec((1, mem_len, n), lambda l, b: (l, b, 0)),
        compiler_params=_params(("arbitrary", "arbitrary")),
        name="memkv",
    )(mem2d, g.reshape(n_layers, 1, d), w)


NP_CHUNK = 256


def _normproj_kernel(x_ref, g_ref, w_ref, out_ref, *f32_refs, scaled_cols, col_scale, f32_cols):
    xn = _rms(x_ref[...], g_ref[...]).astype(BF16)
    n = w_ref.shape[1]
    for c0 in range(0, n, NP_CHUNK):
        y = _dot(xn, w_ref[:, c0:c0 + NP_CHUNK])
        if c0 < scaled_cols:
            y = y * col_scale
        out_ref[:, c0:c0 + NP_CHUNK] = y.astype(BF16)
        if f32_cols is not None and f32_cols[0] <= c0 < f32_cols[1]:
            f32_refs[0][:, c0 - f32_cols[0]:c0 - f32_cols[0] + NP_CHUNK] = y


def _normproj(x2d, g, w_bf16, *, tm, scaled_cols=0, col_scale=1.0, f32_cols=None):
    t, d = x2d.shape
    n = w_bf16.shape[1]
    assert n % NP_CHUNK == 0 and scaled_cols % NP_CHUNK == 0
    out_shape = [jax.ShapeDtypeStruct((t, n), BF16)]
    out_specs = [pl.BlockSpec((tm, n), lambda i: (i, 0))]
    if f32_cols is not None:
        assert f32_cols[0] % NP_CHUNK == 0 and f32_cols[1] % NP_CHUNK == 0
        wf = f32_cols[1] - f32_cols[0]
        out_shape.append(jax.ShapeDtypeStruct((t, wf), F32))
        out_specs.append(pl.BlockSpec((tm, wf), lambda i: (i, 0)))
    outs = pl.pallas_call(
        functools.partial(_normproj_kernel, scaled_cols=scaled_cols, col_scale=col_scale, f32_cols=f32_cols),
        out_shape=out_shape,
        grid=(t // tm,),
        in_specs=[
            pl.BlockSpec((tm, d), lambda i: (i, 0)),
            pl.BlockSpec((1, d), lambda i: (0, 0)),
            pl.BlockSpec((d, n), lambda i: (0, 0)),
        ],
        out_specs=out_specs,
        compiler_params=_params(("parallel",)),
        name="normproj",
    )(x2d, g.reshape(1, d), w_bf16)
    return outs if f32_cols is not None else outs[0]


ATT_ROWS = 32


def _fold8(x, op):
    out = x[0:8]
    for r in range(8, x.shape[0], 8):
        out = op(out, x[r:r + 8])
    return out


def _attn_kernel(dec_ref, q_ref, k_ref, v_ref, bias_ref, lamv_ref, g_ref, o_ref,
                 vt_ref, s0_ref, s1_ref, mx0_ref, mx1_ref, p_ref, st_ref, acc_ref, *, tq, lam_init):
    h = pl.program_id(1)
    nq = q_ref.shape[0] // tq
    n_pairs = nq * (nq + 1) // 2
    assert n_pairs % 2 == 0
    s_refs, mx_refs = (s0_ref, s1_ref), (mx0_ref, mx1_ref)

    for j in range(nq):
        vt_ref[j] = v_ref[j * tq:(j + 1) * tq, :].astype(F32).T.astype(BF16)
    for ref in (s1_ref, mx1_ref, acc_ref, st_ref):
        ref[...] = jnp.zeros_like(ref)

    dec = dec_ref[h]
    lo_map = lax.broadcasted_iota(jnp.int32, (tq, LANES), 1) < DA_HEAD_DIM
    lv = lamv_ref[...]
    lam = (jnp.exp(jnp.sum(lv[0:1] * lv[1:2], axis=-1, keepdims=True))
           - jnp.exp(jnp.sum(lv[2:3] * lv[3:4], axis=-1, keepdims=True)) + lam_init)
    out_gain = g_ref[...] * (1.0 - lam_init)

    def step(slot, pair_a, pair_b):
        (qa, ka), (qb, kb) = pair_a, pair_b
        s_a, s_b = s_refs[slot], s_refs[1 - slot]

        q = q_ref[pl.ds(pl.multiple_of(qa * tq, tq), tq), :]
        kt = k_ref[pl.ds(pl.multiple_of(ka * tq, tq), tq), :]
        zero = jnp.zeros_like(q)
        bias = bias_ref[0, (ka == qa).astype(jnp.int32)]
        for mi, qz in enumerate((jnp.where(lo_map, q, zero), jnp.where(lo_map, zero, q))):
            s = _dot_nt(kt, qz) + bias
            s_a[mi] = s
            mx_refs[slot][mi] = _fold8(s, jnp.maximum)

        first = kb == 0
        vt = vt_ref[kb]
        for mi in range(2):
            m_old = jnp.where(first, MASK_VALUE, st_ref[mi, 0:1, :])
            l_old = jnp.where(first, 0.0, st_ref[mi, 1:2, :])
            mp = m_old - dec
            mn = jnp.maximum(mp, jnp.max(mx_refs[1 - slot][mi], axis=0, keepdims=True))
            alpha = jnp.exp2(mp - mn)
            l8 = None
            for c in range(0, tq, ATT_ROWS):
                p = jnp.exp2(s_b[mi, c:c + ATT_ROWS, :] - mn)
                p_ref[mi, c:c + ATT_ROWS, :] = p.astype(BF16)
                cs = _fold8(p, jnp.add)
                l8 = cs if l8 is None else l8 + cs
            st_ref[mi, 0:1, :] = mn
            st_ref[mi, 1:2, :] = alpha * l_old + jnp.sum(l8, axis=0, keepdims=True)
            kept = jnp.where(first, 0.0, alpha * acc_ref[mi])
            acc_ref[mi] = kept + _dot(vt, p_ref[mi])

        @pl.when(kb == qb)
        def _():
            ot = acc_ref[0] / st_ref[0, 1:2, :] - lam * (acc_ref[1] / st_ref[1, 1:2, :])
            ot = ot * lax.rsqrt(jnp.mean(ot * ot, axis=0, keepdims=True) + EPS) * out_gain
            o_ref[pl.ds(pl.multiple_of(qb * tq, tq), tq), :] = ot.T.astype(BF16)

    def advance(pair):
        qx, kx = pair
        row_done = kx == qx
        last = jnp.logical_and(row_done, qx == nq - 1)
        return (jnp.where(jnp.logical_and(row_done, jnp.logical_not(last)), qx + 1, qx),
                jnp.where(last, kx, jnp.where(row_done, 0, kx + 1)))

    def body(_, pairs):
        pa, pb = pairs
        step(0, pa, pb)
        pa2 = advance(pa)
        step(1, pa2, pa)
        return (advance(pa2), pa2)

    origin = (jnp.int32(0), jnp.int32(0))
    pa, pb = lax.fori_loop(0, n_pairs // 2, body, (origin, origin))
    step(0, pa, pb)


def _alibi_tiles(tq):
    slopes = 2.0 ** (-8.0 * np.arange(1, DA_HEADS + 1) / DA_HEADS)
    kpos = np.arange(tq)[:, None]
    qpos = np.arange(tq)[None, :]
    low = slopes[:, None, None] * LOG2E * (kpos - qpos)[None]
    diag = np.where((kpos // CHUNK <= qpos // CHUNK)[None],
                    -slopes[:, None, None] * LOG2E * np.abs(qpos - kpos)[None], MASK_VALUE)
    dec = slopes * LOG2E * tq
    return jnp.asarray(dec, F32), jnp.asarray(np.stack([low, diag], axis=1), F32)


def _diff_attention(proj, lamv, subln_g, *, batch, seq, tq, lam_init):
    t = proj.shape[0]
    nq = seq // tq
    dec, bias = _alibi_tiles(tq)
    grid_spec = pltpu.PrefetchScalarGridSpec(
        num_scalar_prefetch=1,
        grid=(batch, DA_HEADS),
        in_specs=[
            pl.BlockSpec((seq, LANES), lambda b, h, dec: (b, h)),
            pl.BlockSpec((seq, LANES), lambda b, h, dec: (b, DA_HEADS + h)),
            pl.BlockSpec((seq, LANES), lambda b, h, dec: (b, 2 * DA_HEADS + h)),
            pl.BlockSpec((1, 2, tq, tq), lambda b, h, dec: (h, 0, 0, 0)),
            pl.BlockSpec((4, DA_HEAD_DIM), lambda b, h, dec: (0, 0)),
            pl.BlockSpec((LANES, 1), lambda b, h, dec: (0, 0)),
        ],
        out_specs=pl.BlockSpec((seq, LANES), lambda b, h, dec: (b, h)),
        scratch_shapes=[
            pltpu.VMEM((nq, LANES, tq), BF16),
            pltpu.VMEM((2, tq, tq), F32),
            pltpu.VMEM((2, tq, tq), F32),
            pltpu.VMEM((2, 8, tq), F32),
            pltpu.VMEM((2, 8, tq), F32),
            pltpu.VMEM((2, tq, tq), BF16),
            pltpu.VMEM((2, 8, tq), F32),
            pltpu.VMEM((2, LANES, tq), F32),
        ],
    )
    return pl.pallas_call(
        functools.partial(_attn_kernel, tq=tq, lam_init=lam_init),
        out_shape=jax.ShapeDtypeStruct((t, DA_WIDTH), BF16),
        grid_spec=grid_spec,
        compiler_params=_params(("parallel", "parallel")),
        name="diff_attn",
    )(dec, proj, proj, proj, bias, lamv, subln_g.reshape(LANES, 1))


HG_HEADS_PER_STEP = 3


def _hgrn_kernel(q_ref, fz_ref, v_ref, gate_ref, lbl_ref, g_ref, o_ref, st_ref, *, layer, n_chunks):
    @pl.when(pl.program_id(2) == 0)
    def _():
        st_ref[...] = jnp.zeros_like(st_ref)

    rr = lax.broadcasted_iota(jnp.int32, (CHUNK, CHUNK), 0)
    cc = lax.broadcasted_iota(jnp.int32, (CHUNK, CHUNK), 1)
    tril = (cc <= rr).astype(F32)
    sub_row = lax.broadcasted_iota(jnp.int32, (SUB_BLOCK, LANES), 0)
    n_sub = CHUNK // SUB_BLOCK
    gain = g_ref[...]
    r2 = lax.broadcasted_iota(jnp.int32, (2 * CHUNK, CHUNK), 0)
    c2 = lax.broadcasted_iota(jnp.int32, (2 * CHUNK, CHUNK), 1)
    in_chunk = jnp.logical_and(r2 < CHUNK, c2 <= r2)
    in_sub = jnp.logical_and(jnp.logical_and(r2 >= CHUNK, c2 <= r2 - CHUNK),
                             c2 // SUB_BLOCK == (r2 - CHUNK) // SUB_BLOCK)
    cum_w = jnp.where(jnp.logical_or(in_chunk, in_sub), 1.0, 0.0).astype(BF16)
    blk_r, blk_c = rr // SUB_BLOCK, cc // SUB_BLOCK
    mask_sub = jnp.logical_and(blk_r == blk_c, cc <= rr)
    mask_16 = jnp.logical_and(blk_r % 2 == 1, blk_c == blk_r - 1)
    zeros16 = jnp.zeros((SUB_BLOCK, LANES), F32)
    zeros32 = jnp.zeros((2 * SUB_BLOCK, LANES), F32)

    def head_lanes(hh):
        return slice(hh * LANES, (hh + 1) * LANES)

    def lower_bound(hh):
        lbl = lbl_ref[:, head_lanes(hh)]
        e = jnp.exp(lbl - jnp.max(lbl, axis=0, keepdims=True))
        pr = e / jnp.sum(e, axis=0, keepdims=True)
        return jnp.sum(pr[0:layer + 1], axis=0, keepdims=True) - pr[0:1]

    def prepare(hh):
        lb = lower_bound(hh)
        z_all = fz_ref[:, head_lanes(hh)]
        f_all = lb + (1.0 - lb) * (1.0 / (1.0 + jnp.exp(-z_all)))
        k_all = (1.0 - lb) * (1.0 / (1.0 + jnp.exp(z_all)))
        logf = jnp.log(f_all)
        lf_hi = logf.astype(BF16)
        lf_r1 = logf - lf_hi.astype(F32)
        lf_mid = lf_r1.astype(BF16)
        lf_lo = (lf_r1 - lf_mid.astype(F32)).astype(BF16)
        lf3 = jnp.concatenate([lf_hi, lf_mid, lf_lo], axis=1)
        cums = []
        for c in range(n_chunks):
            e3 = _dot(cum_w, lf3[c * CHUNK:(c + 1) * CHUNK])
            cums.append(e3[:, 0:LANES] + e3[:, LANES:2 * LANES] + e3[:, 2 * LANES:3 * LANES])
        min_rel = cums[0][CHUNK:]
        for c in range(1, n_chunks):
            min_rel = jnp.minimum(min_rel, cums[c][CHUNK:])
        return k_all, cums, jnp.min(min_rel)

    prepared = [prepare(hh) for hh in range(HG_HEADS_PER_STEP)]
    worst = prepared[0][2]
    for _, _, m in prepared[1:]:
        worst = jnp.minimum(worst, m)
    safe = worst >= HG_SAFE_LOG_DECAY

    def fast_chunk(hh, c, st):
        k_all, cums, _ = prepared[hh]
        lanes = head_lanes(hh)
        rows = slice(c * CHUNK, (c + 1) * CHUNK)
        q = q_ref[rows, lanes].astype(F32)
        vb = v_ref[rows, lanes]
        k = k_all[rows]
        b, rel = cums[c][:CHUNK], cums[c][CHUNK:]
        b15, b31, b47, b63 = b[15:16], b[31:32], b[47:48], b[63:64]
        q32 = jnp.concatenate([zeros32, q[32:64] * jnp.exp(b[32:64] - b31)], axis=0)
        k32 = jnp.concatenate([k[0:32] * jnp.exp(b31 - b[0:32]), zeros32], axis=0)
        q16 = jnp.concatenate([zeros16, q[16:32] * jnp.exp(b[16:32] - b15),
                               zeros16, q[48:64] * jnp.exp(b[48:64] - b47)], axis=0)
        k16 = jnp.concatenate([k[0:16] * jnp.exp(b15 - b[0:16]), zeros16,
                               k[32:48] * jnp.exp(b47 - b[32:48]), zeros16], axis=0)
        attn = (_dot_nt(q32.astype(BF16), k32.astype(BF16))
                + jnp.where(mask_16, _dot_nt(q16.astype(BF16), k16.astype(BF16)), 0.0)
                + jnp.where(mask_sub, _dot_nt((q * jnp.exp(rel)).astype(BF16),
                                              (k * jnp.exp(-rel)).astype(BF16)), 0.0))
        o = _dot(attn.astype(BF16), vb) + _dot_nt((q * jnp.exp(b)).astype(BF16), st.astype(BF16))
        upd = lax.dot_general(vb, (k * jnp.exp(b63 - b)).astype(BF16), (((0,), (0,)), ((), ())),
                              preferred_element_type=F32)
        gt = gate_ref[rows, lanes].astype(F32)
        o_ref[rows, lanes] = (_rms(o, gain) * (gt * (1.0 / (1.0 + jnp.exp(-gt))))).astype(BF16)
        return st * jnp.exp(b63) + upd

    @pl.when(safe)
    def _():
        states = [st_ref[hh] for hh in range(HG_HEADS_PER_STEP)]
        for c in range(n_chunks):
            states = [fast_chunk(hh, c, st) for hh, st in enumerate(states)]
        for hh, st in enumerate(states):
            st_ref[hh] = st

    def exact_chunk(hh, lb, c):
        lanes = head_lanes(hh)
        rows = pl.ds(pl.multiple_of(c * CHUNK, CHUNK), CHUNK)
        z = fz_ref[rows, lanes]
        q = q_ref[rows, lanes].astype(F32)
        v = v_ref[rows, lanes].astype(F32)
        f = lb + (1.0 - lb) * (1.0 / (1.0 + jnp.exp(-z)))
        k = (1.0 - lb) * (1.0 / (1.0 + jnp.exp(z)))
        b = jnp.dot(tril, jnp.log(f), preferred_element_type=F32, precision=lax.Precision.HIGHEST)
        vb = v.astype(BF16)
        st = st_ref[hh]
        o_inter = _dot_nt((q * jnp.exp(b)).astype(BF16), st.astype(BF16))

        outs = []
        for i in range(n_sub):
            r0 = i * SUB_BLOCK
            qi = q[r0:r0 + SUB_BLOCK]
            ki = k[r0:r0 + SUB_BLOCK]
            bi = b[r0:r0 + SUB_BLOCK]
            vi = v[r0:r0 + SUB_BLOCK]
            oi = jnp.zeros((SUB_BLOCK, LANES), F32)
            for s in range(SUB_BLOCK):
                dcy = jnp.where(sub_row >= s, jnp.exp(bi - bi[s:s + 1]), 0.0)
                a_s = jnp.sum(dcy * qi * ki[s:s + 1], axis=-1, keepdims=True)
                oi = oi + a_s * vi[s:s + 1]
            if i > 0:
                ref_b = b[r0 - 1:r0]
                qd = (qi * jnp.exp(bi - ref_b)).astype(BF16)
                kd = (k[0:r0] * jnp.exp(ref_b - b[0:r0])).astype(BF16)
                oi = oi + _dot(_dot_nt(qd, kd).astype(BF16), vb[0:r0])
            outs.append(oi)
        o = jnp.concatenate(outs, axis=0) + o_inter

        b_last = b[CHUNK - 1:CHUNK]
        kd = (k * jnp.exp(b_last - b)).astype(BF16)
        upd = lax.dot_general(vb, kd, (((0,), (0,)), ((), ())), preferred_element_type=F32)
        st_ref[hh] = st * jnp.exp(b_last) + upd

        gt = gate_ref[rows, lanes].astype(F32)
        o_ref[rows, lanes] = (_rms(o, gain) * (gt * (1.0 / (1.0 + jnp.exp(-gt))))).astype(BF16)

    @pl.when(jnp.logical_not(safe))
    def _():
        for hh in range(HG_HEADS_PER_STEP):
            lb = lower_bound(hh)
            lax.fori_loop(0, n_chunks, lambda c, carry, hh=hh, lb=lb: (exact_chunk(hh, lb, c), carry)[1], 0)


def _hgrn(proj, fz32, lb_logits, out_norm_g, *, batch, seq, ts, layer):
    t = proj.shape[0]
    ns = seq // ts
    n_b = lb_logits.shape[0]
    hp = HG_HEADS_PER_STEP
    assert HG_HEADS % hp == 0
    groups = HG_HEADS // hp
    wide = hp * LANES
    return pl.pallas_call(
        functools.partial(_hgrn_kernel, layer=layer, n_chunks=ts // CHUNK),
        out_shape=jax.ShapeDtypeStruct((t, HG_VAL_WIDTH), BF16),
        grid=(batch, groups, ns),
        in_specs=[
            pl.BlockSpec((ts, wide), lambda b, h, i: (b * ns + i, h)),
            pl.BlockSpec((ts, wide), lambda b, h, i: (b * ns + i, h)),
            pl.BlockSpec((ts, wide), lambda b, h, i: (b * ns + i, 2 * groups + h)),
            pl.BlockSpec((ts, wide), lambda b, h, i: (b * ns + i, 3 * groups + h)),
            pl.BlockSpec((n_b, wide), lambda b, h, i: (0, h)),
            pl.BlockSpec((1, LANES), lambda b, h, i: (0, 0)),
        ],
        out_specs=pl.BlockSpec((ts, wide), lambda b, h, i: (b * ns + i, h)),
        scratch_shapes=[pltpu.VMEM((hp, HG_VAL_DIM, HG_KEY_DIM), F32)],
        compiler_params=_params(("parallel", "parallel", "arbitrary")),
        name="hgrn2",
    )(proj, fz32, proj, proj, lb_logits, out_norm_g.reshape(1, LANES))


def _outproj_kernel(o_ref, qm_ref, kv_ref, x_ref, w_ref, out_ref):
    qm = qm_ref[...]
    kmem = kv_ref[0, :, 0:MEM_WIDTH]
    vmem = kv_ref[0, :, MEM_WIDTH:2 * MEM_WIDTH]
    head = lax.broadcasted_iota(jnp.int32, qm.shape, 1) // MEM_HEAD_DIM
    om = jnp.zeros(qm.shape, F32)
    for hh in range(MEM_HEADS):
        qh = jnp.where(head == hh, qm, jnp.zeros_like(qm))
        s = _dot_nt(qh, kmem) * (MEM_HEAD_DIM ** -0.5)
        p = jnp.exp(s - jnp.max(s, axis=-1, keepdims=True))
        p = p / jnp.sum(p, axis=-1, keepdims=True)
        om = jnp.where(head == hh, _dot(p.astype(BF16), vmem), om)
    wo = o_ref.shape[1]
    out_ref[...] = (x_ref[...] + _dot(o_ref[...], w_ref[0:wo, :])
                    + _dot(om.astype(BF16), w_ref[wo:wo + MEM_WIDTH, :]))


def _outproj(o, proj, qm_block, memkv, layer, x2d, w_out_bf16, *, tm, seq, mem_len):
    t, d = x2d.shape
    wo = o.shape[1]
    tiles_per_batch = seq // tm
    return pl.pallas_call(
        _outproj_kernel,
        out_shape=jax.ShapeDtypeStruct((t, d), F32),
        grid=(t // tm,),
        in_specs=[
            pl.BlockSpec((tm, wo), lambda i: (i, 0)),
            pl.BlockSpec((tm, MEM_WIDTH), lambda i: (i, qm_block)),
            pl.BlockSpec((1, mem_len, 2 * MEM_WIDTH), lambda i: (layer, i // tiles_per_batch, 0)),
            pl.BlockSpec((tm, d), lambda i: (i, 0)),
            pl.BlockSpec((wo + MEM_WIDTH, d), lambda i: (0, 0)),
        ],
        out_specs=pl.BlockSpec((tm, d), lambda i: (i, 0)),
        compiler_params=_params(("parallel",)),
        name="outproj",
    )(o, proj, memkv, x2d, w_out_bf16)


MXU_COLS = 256


def _ffn_kernel(te_ref, nt_ref, x_ref, g_ref, wg_ref, wu_ref, wd_ref, out_ref, *, dense):
    del te_ref
    i = pl.program_id(0)
    f = wd_ref.shape[1]
    half = f // 2 // MXU_COLS * MXU_COLS

    @pl.when(i < nt_ref[0])
    def _():
        x = x_ref[...]
        xn = (_rms(x, g_ref[...]) if dense else x).astype(BF16)
        y = x if dense else None
        for c0, c1 in ((0, half), (half, f)):
            gt = _dot(xn, wg_ref[0, :, c0:c1])
            up = _dot(xn, wu_ref[0, :, c0:c1])
            act = (gt * (1.0 / (1.0 + jnp.exp(-gt))) * up).astype(BF16)
            part = _dot(act, wd_ref[0, c0:c1, :])
            y = part if y is None else y + part
        out_ref[...] = y

    @pl.when(i >= nt_ref[0])
    def _():
        out_ref[...] = jnp.zeros_like(out_ref)


def _ffn(tile_expert, n_tiles, x2d, g, w_gate_up_bf16, w_down_bf16, *, tm, dense):
    r, d = x2d.shape
    f = w_down_bf16.shape[1]
    resident = pl.Buffered(1)
    grid_spec = pltpu.PrefetchScalarGridSpec(
        num_scalar_prefetch=2,
        grid=(r // tm,),
        in_specs=[
            pl.BlockSpec((tm, d), lambda i, te, nt: (jnp.minimum(i, nt[0] - 1), 0)),
            pl.BlockSpec((1, d), lambda i, te, nt: (0, 0)),
            pl.BlockSpec((1, d, f), lambda i, te, nt: (te[i], 0, 0), pipeline_mode=resident),
            pl.BlockSpec((1, d, f), lambda i, te, nt: (te[i], 0, 1), pipeline_mode=resident),
            pl.BlockSpec((1, f, d), lambda i, te, nt: (te[i], 0, 0), pipeline_mode=resident),
        ],
        out_specs=pl.BlockSpec((tm, d), lambda i, te, nt: (i, 0)),
    )
    return pl.pallas_call(
        functools.partial(_ffn_kernel, dense=dense),
        out_shape=jax.ShapeDtypeStruct((r, d), F32),
        grid_spec=grid_spec,
        compiler_params=_params(("parallel",)),
        name="swiglu_dense" if dense else "swiglu_grouped",
    )(tile_expert, n_tiles, x2d, g.reshape(1, d), w_gate_up_bf16, w_gate_up_bf16, w_down_bf16)


META_E1, META_E2, META_W1, META_W2, META_R1, META_R2 = range(6)


def _router_kernel(x_ref, g_ref, wr_ref, meta_ref, cnt_ref, run_ref):
    @pl.when(pl.program_id(0) == 0)
    def _():
        run_ref[...] = jnp.zeros_like(run_ref)

    tm = x_ref.shape[0]
    xn = _rms(x_ref[...], g_ref[...])
    x_hi = xn.astype(BF16)
    x_lo = (xn - x_hi.astype(F32)).astype(BF16)
    w = wr_ref[...]
    w_hi = w.astype(BF16)
    w_lo = (w - w_hi.astype(F32)).astype(BF16)
    logits = _dot(x_hi, w_hi) + (_dot(x_lo, w_hi) + _dot(x_hi, w_lo))
    lane = lax.broadcasted_iota(jnp.int32, logits.shape, 1)
    logits = jnp.where(lane < N_EXPERTS, logits, -jnp.inf)

    def top(vals):
        best = jnp.max(vals, axis=-1, keepdims=True)
        idx = jnp.min(jnp.where(vals == best, lane, LANES), axis=-1, keepdims=True)
        return best, idx

    v1, e1 = top(logits)
    v2, e2 = top(jnp.where(lane == e1, -jnp.inf, logits))
    ex = jnp.exp(v2 - v1)
    w1 = 1.0 / (1.0 + ex)
    w2 = ex / (1.0 + ex)

    oh1 = (lane == e1).astype(F32)
    oh2 = (lane == e2).astype(F32)
    rr = lax.broadcasted_iota(jnp.int32, (tm, tm), 0)
    cc = lax.broadcasted_iota(jnp.int32, (tm, tm), 1)
    strict = (cc < rr).astype(BF16)
    before = _dot(strict, (oh1 + oh2).astype(BF16)) + run_ref[0:1, :]
    r1 = jnp.sum(before * oh1, axis=-1, keepdims=True)
    r2 = jnp.sum(before * oh2, axis=-1, keepdims=True)
    run_ref[0:1, :] = run_ref[0:1, :] + jnp.sum(oh1 + oh2, axis=0, keepdims=True)

    meta = jnp.zeros(logits.shape, F32)
    for slot, val in ((META_E1, e1.astype(F32)), (META_E2, e2.astype(F32)), (META_W1, w1), (META_W2, w2),
                      (META_R1, r1), (META_R2, r2)):
        meta = jnp.where(lane == slot, val, meta)
    meta_ref[...] = meta
    cnt_ref[...] = run_ref[...]


def _router(x2d, g, w_router_padded, *, tm):
    t, d = x2d.shape
    return pl.pallas_call(
        _router_kernel,
        out_shape=[jax.ShapeDtypeStruct((t, LANES), F32), jax.ShapeDtypeStruct((8, LANES), F32)],
        grid=(t // tm,),
        in_specs=[
            pl.BlockSpec((tm, d), lambda i: (i, 0)),
            pl.BlockSpec((1, d), lambda i: (0, 0)),
            pl.BlockSpec((d, LANES), lambda i: (0, 0)),
        ],
        out_specs=[pl.BlockSpec((tm, LANES), lambda i: (i, 0)), pl.BlockSpec((8, LANES), lambda i: (0, 0))],
        scratch_shapes=[pltpu.VMEM((8, LANES), F32)],
        compiler_params=_params(("arbitrary",)),
        name="moe_router",
    )(x2d, g.reshape(1, d), w_router_padded)


ROW_DMA_UNROLL = 8


def _dispatch_kernel(slot_ref, pad_ref, x_ref, g_ref, xs_ref, xn_ref, zrow_ref, sem):
    tm = x_ref.shape[0]

    @pl.when(pl.program_id(0) == 0)
    def _():
        zrow_ref[...] = jnp.zeros_like(zrow_ref)

        def zero_copy(r):
            return pltpu.make_async_copy(zrow_ref.at[pl.ds(0, 1)], xs_ref.at[pl.ds(r, 1)], sem)

        def zero_block(r8):
            rows = pl.ds(pl.multiple_of(r8 * 8, 8), 8)
            return pltpu.make_async_copy(zrow_ref, xs_ref.at[rows], sem)

        tail = (pad_ref[0, 2 * N_EXPERTS - 1] // 8, xs_ref.shape[0] // 8)
        for e in range(N_EXPERTS):
            lo, hi = pad_ref[0, e], pad_ref[0, N_EXPERTS + e]
            lax.fori_loop(lo, hi, lambda r, c: (zero_copy(r).start(), c)[1], 0)
        lax.fori_loop(tail[0], tail[1], lambda r8, c: (zero_block(r8).start(), c)[1], 0)
        for e in range(N_EXPERTS):
            lo, hi = pad_ref[0, e], pad_ref[0, N_EXPERTS + e]
            lax.fori_loop(lo, hi, lambda r, c: (zero_copy(r).wait(), c)[1], 0)
        lax.fori_loop(tail[0], tail[1], lambda r8, c: (zero_block(r8).wait(), c)[1], 0)

    xn_ref[...] = _rms(x_ref[...], g_ref[...])

    def row_copy(r, k):
        return pltpu.make_async_copy(xn_ref.at[pl.ds(r, 1)], xs_ref.at[pl.ds(slot_ref[0, 0, 2 * r + k], 1)], sem)

    def start(r, _):
        row_copy(r, 0).start()
        row_copy(r, 1).start()
        return 0

    def wait(r, _):
        row_copy(r, 0).wait()
        row_copy(r, 1).wait()
        return 0

    lax.fori_loop(0, tm, start, 0, unroll=ROW_DMA_UNROLL)
    lax.fori_loop(0, tm, wait, 0, unroll=ROW_DMA_UNROLL)


def _dispatch(slots3, pad_rows, x2d, g, n_slots, *, tm):
    t, d = x2d.shape
    return pl.pallas_call(
        _dispatch_kernel,
        out_shape=jax.ShapeDtypeStruct((n_slots, d), F32),
        grid=(t // tm,),
        in_specs=[
            pl.BlockSpec((1, 1, 2 * tm), lambda i: (i, 0, 0), memory_space=pltpu.SMEM),
            pl.BlockSpec((1, 2 * N_EXPERTS), lambda i: (0, 0), memory_space=pltpu.SMEM),
            pl.BlockSpec((tm, d), lambda i: (i, 0)),
            pl.BlockSpec((1, d), lambda i: (0, 0)),
        ],
        out_specs=pl.BlockSpec(memory_space=pl.ANY),
        scratch_shapes=[pltpu.VMEM((tm, d), F32), pltpu.VMEM((8, d), F32), pltpu.SemaphoreType.DMA],
        compiler_params=_params(("arbitrary",)),
        name="moe_dispatch",
    )(slots3, pad_rows, x2d, g.reshape(1, d))


def _combine_kernel(slot_ref, x_ref, meta_ref, y_ref, g_ref, out_ref, ybuf_ref, sem, *, final_norm):
    tm = x_ref.shape[0]

    def row_copy(r, k):
        return pltpu.make_async_copy(y_ref.at[pl.ds(slot_ref[0, 0, 2 * r + k], 1)], ybuf_ref.at[k, pl.ds(r, 1)], sem)

    def start(r, _):
        row_copy(r, 0).start()
        row_copy(r, 1).start()
        return 0

    def wait(r, _):
        row_copy(r, 0).wait()
        row_copy(r, 1).wait()
        return 0

    lax.fori_loop(0, tm, start, 0, unroll=ROW_DMA_UNROLL)
    lax.fori_loop(0, tm, wait, 0, unroll=ROW_DMA_UNROLL)
    meta = meta_ref[...]
    w1 = meta[:, META_W1:META_W1 + 1]
    w2 = meta[:, META_W2:META_W2 + 1]
    out = x_ref[...] + (w1 * ybuf_ref[0] + w2 * ybuf_ref[1])
    out_ref[...] = _rms(out, g_ref[...]) if final_norm else out


def _combine(slots3, x2d, meta, y, g, *, tm, final_norm):
    t, d = x2d.shape
    return pl.pallas_call(
        functools.partial(_combine_kernel, final_norm=final_norm),
        out_shape=jax.ShapeDtypeStruct((t, d), F32),
        grid=(t // tm,),
        in_specs=[
            pl.BlockSpec((1, 1, 2 * tm), lambda i: (i, 0, 0), memory_space=pltpu.SMEM),
            pl.BlockSpec((tm, d), lambda i: (i, 0)),
            pl.BlockSpec((tm, LANES), lambda i: (i, 0)),
            pl.BlockSpec(memory_space=pl.ANY),
            pl.BlockSpec((1, d), lambda i: (0, 0)),
        ],
        out_specs=pl.BlockSpec((tm, d), lambda i: (i, 0)),
        scratch_shapes=[pltpu.VMEM((2, tm, d), F32), pltpu.SemaphoreType.DMA],
        compiler_params=_params(("arbitrary",)),
        name="moe_combine",
    )(slots3, x2d, meta, y, g.reshape(1, d))


def _moe(x2d, norm_g, w_router, w_gate_up_bf16, w_down_bf16, first_expert, final_g,
         *, tm_route, tm_row, tm_ffn, final_norm):
    t, d = x2d.shape
    wr = jnp.zeros((d, LANES), F32).at[:, :N_EXPERTS].set(w_router)
    meta, counts = _router(x2d, norm_g, wr, tm=tm_route)

    cnt = counts[0, :N_EXPERTS].astype(jnp.int32)
    padded = (cnt + tm_ffn - 1) // tm_ffn * tm_ffn
    ends = jnp.cumsum(padded)
    starts = ends - padded
    e12 = meta[:, META_E1:META_E2 + 1].astype(jnp.int32)
    r12 = meta[:, META_R1:META_R2 + 1].astype(jnp.int32)
    slots = starts[e12] + r12
    slots3 = slots.reshape(t // tm_row, 1, 2 * tm_row)
    n_tiles_max = (2 * t) // tm_ffn + N_EXPERTS
    tile_start = jnp.arange(n_tiles_max, dtype=jnp.int32) * tm_ffn
    tile_expert = jnp.minimum(
        jnp.sum((ends[None, :] <= tile_start[:, None]).astype(jnp.int32), axis=1), N_EXPERTS - 1)
    n_tiles = (ends[N_EXPERTS - 1] // tm_ffn).astype(jnp.int32).reshape(1)

    pad_rows = jnp.concatenate([starts + cnt, ends]).astype(jnp.int32).reshape(1, 2 * N_EXPERTS)
    xs = _dispatch(slots3, pad_rows, x2d, norm_g, n_tiles_max * tm_ffn, tm=tm_row)
    ys = _ffn(tile_expert + first_expert, n_tiles, xs, norm_g, w_gate_up_bf16, w_down_bf16,
              tm=tm_ffn, dense=False)
    return _combine(slots3, x2d, meta, ys, final_g, tm=tm_row, final_norm=final_norm)


def _tiles(batch, seq):
    full = seq >= 4096
    return dict(
        tm_proj=512 if full else 128,
        tq=512 if full else 128,
        ts=512 if full else 128,
        tm_out=512 if full else 128,
        tm_dense=512 if full else 128,
        tm_route=512 if full else 128,
        tm_row=256 if full else 128,
        tm_ffn=512 if full else 128,
    )


def kernel(x, mem, a_norm_mix, a_w_in, a_lam_q1, a_lam_k1, a_lam_q2, a_lam_k2, a_subln, a_mem_norm, a_w_mem_kv, a_w_out, b_norm_mix, b_w_in, b_lb_logits, b_out_norm, b_mem_norm, b_w_mem_kv, b_w_out, dense_norm, dense_w_gate_up, dense_w_down, moe_norm, moe_router, moe_w_gate_up, moe_w_down, final_norm):
    batch, seq, d = x.shape
    mem_len = mem.shape[1]
    t = batch * seq
    ts = _tiles(batch, seq)
    x2d = x.reshape(t, d)
    mem2d = mem.reshape(batch * mem_len, d)
    kv_a = _memkv(mem2d, a_mem_norm, a_w_mem_kv, mem_len)
    kv_b = _memkv(mem2d, b_mem_norm, b_w_mem_kv, mem_len)
    all_tiles = jnp.full((1,), t // ts["tm_dense"], jnp.int32)
    dense_gu, dense_dn = dense_w_gate_up.astype(BF16), dense_w_down.astype(BF16)
    moe_gu = moe_w_gate_up.astype(BF16).reshape((-1,) + moe_w_gate_up.shape[2:])
    moe_dn = moe_w_down.astype(BF16).reshape((-1,) + moe_w_down.shape[2:])

    for i in range(DEPTH):
        j = i // N_MIXERS
        if i % N_MIXERS == 0:
            lam_init = 0.8 - 0.6 * float(np.exp(-0.3 * i))
            proj = _normproj(x2d, a_norm_mix[j], a_w_in[j].astype(BF16), tm=ts["tm_proj"],
                             scaled_cols=DA_WIDTH, col_scale=DA_HEAD_DIM ** -0.5 * LOG2E)
            lamv = jnp.stack([a_lam_q1[j], a_lam_k1[j], a_lam_q2[j], a_lam_k2[j]])
            o = _diff_attention(proj, lamv, a_subln[j], batch=batch, seq=seq, tq=ts["tq"], lam_init=lam_init)
            x2d = _outproj(o, proj, 3 * DA_WIDTH // MEM_WIDTH, kv_a, j, x2d, a_w_out[j].astype(BF16),
                           tm=ts["tm_out"], seq=seq, mem_len=mem_len)
        else:
            proj, fz32 = _normproj(x2d, b_norm_mix[j], b_w_in[j].astype(BF16), tm=ts["tm_proj"],
                                   f32_cols=(HG_KEY_WIDTH, 2 * HG_KEY_WIDTH))
            o = _hgrn(proj, fz32, b_lb_logits, b_out_norm[j], batch=batch, seq=seq, ts=ts["ts"], layer=j)
            x2d = _outproj(o, proj, (2 * HG_KEY_WIDTH + 2 * HG_VAL_WIDTH) // MEM_WIDTH, kv_b, j, x2d,
                           b_w_out[j].astype(BF16), tm=ts["tm_out"], seq=seq, mem_len=mem_len)
        if i % 2 == 0:
            layer_tiles = jnp.full((t // ts["tm_dense"],), j, jnp.int32)
            x2d = _ffn(layer_tiles, all_tiles, x2d, dense_norm[j], dense_gu, dense_dn, tm=ts["tm_dense"], dense=True)
        else:
            x2d = _moe(x2d, moe_norm[j], moe_router[j], moe_gu, moe_dn, j * N_EXPERTS, final_norm,
                       tm_route=ts["tm_route"], tm_row=ts["tm_row"], tm_ffn=ts["tm_ffn"],
                       final_norm=(i == DEPTH - 1))
    return x2d.reshape(batch, seq, d)
```

```python
import functools

import numpy as np
import jax
import jax.numpy as jnp
from jax import lax
from jax.experimental import pallas as pl
from jax.experimental.pallas import tpu as pltpu

F32 = jnp.float32
BF16 = jnp.bfloat16

DEPTH = 4
N_MIXERS = 2
CHUNK = 64
DA_HEADS = 6
DA_HEAD_DIM = 64
DA_WIDTH = DA_HEADS * 2 * DA_HEAD_DIM
HG_HEADS = 6
HG_KEY_DIM = 128
HG_VAL_DIM = 128
HG_KEY_WIDTH = HG_HEADS * HG_KEY_DIM
HG_VAL_WIDTH = HG_HEADS * HG_VAL_DIM
MEM_HEADS = 4
MEM_HEAD_DIM = 64
MEM_WIDTH = MEM_HEADS * MEM_HEAD_DIM
N_EXPERTS = 8
EPS = 1e-6
MASK_VALUE = -1e30
LOG2E = float(np.log2(np.e))

LANES = 128
SUB_BLOCK = 16
HG_SAFE_LOG_DECAY = -64.0
VMEM_LIMIT = 56 * 1024 * 1024


def _params(sem):
    return pltpu.CompilerParams(dimension_semantics=sem, vmem_limit_bytes=VMEM_LIMIT)


def _rms(x, g):
    return x * lax.rsqrt(jnp.mean(x * x, axis=-1, keepdims=True) + EPS) * g


def _dot(a, b):
    return jnp.dot(a, b, preferred_element_type=F32)


def _dot_nt(a, b):
    return lax.dot_general(a, b, (((1,), (1,)), ((), ())), preferred_element_type=F32)


def _memkv_kernel(mem_ref, g_ref, w_ref, out_ref):
    y = _rms(mem_ref[...], g_ref[0]).astype(BF16)
    out_ref[0] = _dot(y, w_ref[0].astype(BF16)).astype(BF16)


def _memkv(mem2d, g, w, mem_len):
    n_layers, d, n = w.shape
    rows = mem2d.shape[0]
    return pl.pallas_call(
        _memkv_kernel,
        out_shape=jax.ShapeDtypeStruct((n_layers, rows, n), BF16),
        grid=(n_layers, rows // mem_len),
        in_specs=[
            pl.BlockSpec((mem_len, d), lambda l, b: (b, 0)),
            pl.BlockSpec((1, 1, d), lambda l, b: (l, 0, 0)),
            pl.BlockSpec((1, d, n), lambda l, b: (l, 0, 0)),
        ],
        out_specs=pl.BlockSpec((1, mem_len, n), lambda l, b: (l, b, 0)),
        compiler_params=_params(("arbitrary", "arbitrary")),
        name="memkv",
    )(mem2d, g.reshape(n_layers, 1, d), w)


NP_CHUNK = 256


def _normproj_kernel(x_ref, g_ref, w_ref, out_ref, *f32_refs, scaled_cols, col_scale, f32_cols):
    xn = _rms(x_ref[...], g_ref[...]).astype(BF16)
    n = w_ref.shape[1]
    for c0 in range(0, n, NP_CHUNK):
        y = _dot(xn, w_ref[:, c0:c0 + NP_CHUNK])
        if c0 < scaled_cols:
            y = y * col_scale
        out_ref[:, c0:c0 + NP_CHUNK] = y.astype(BF16)
        if f32_cols is not None and f32_cols[0] <= c0 < f32_cols[1]:
            f32_refs[0][:, c0 - f32_cols[0]:c0 - f32_cols[0] + NP_CHUNK] = y


def _normproj(x2d, g, w_bf16, *, tm, scaled_cols=0, col_scale=1.0, f32_cols=None):
    t, d = x2d.shape
    n = w_bf16.shape[1]
    assert n % NP_CHUNK == 0 and scaled_cols % NP_CHUNK == 0
    out_shape = [jax.ShapeDtypeStruct((t, n), BF16)]
    out_specs = [pl.BlockSpec((tm, n), lambda i: (i, 0))]
    if f32_cols is not None:
        assert f32_cols[0] % NP_CHUNK == 0 and f32_cols[1] % NP_CHUNK == 0
        wf = f32_cols[1] - f32_cols[0]
        out_shape.append(jax.ShapeDtypeStruct((t, wf), F32))
        out_specs.append(pl.BlockSpec((tm, wf), lambda i: (i, 0)))
    outs = pl.pallas_call(
        functools.partial(_normproj_kernel, scaled_cols=scaled_cols, col_scale=col_scale, f32_cols=f32_cols),
        out_shape=out_shape,
        grid=(t // tm,),
        in_specs=[
            pl.BlockSpec((tm, d), lambda i: (i, 0)),
            pl.BlockSpec((1, d), lambda i: (0, 0)),
            pl.BlockSpec((d, n), lambda i: (0, 0)),
        ],
        out_specs=out_specs,
        compiler_params=_params(("parallel",)),
        name="normproj",
    )(x2d, g.reshape(1, d), w_bf16)
    return outs if f32_cols is not None else outs[0]


ATT_ROWS = 32


def _fold8(x, op):
    out = x[0:8]
    for r in range(8, x.shape[0], 8):
        out = op(out, x[r:r + 8])
    return out


ATT_HEADS_PER_STEP = 2


def _attn_kernel(dec_ref, q_ref, k_ref, v_ref, bias_ref, lamv_ref, g_ref, o_ref,
                 vt_ref, s0_ref, s1_ref, mx0_ref, mx1_ref, p_ref, st_ref, acc_ref, *, tq, lam_init):
    hg = pl.program_id(1)
    heads = range(ATT_HEADS_PER_STEP)
    nq = q_ref.shape[0] // tq
    n_pairs = nq * (nq + 1) // 2
    assert n_pairs % 2 == 0
    s_refs, mx_refs = (s0_ref, s1_ref), (mx0_ref, mx1_ref)

    def head_lanes(hh):
        return slice(hh * LANES, (hh + 1) * LANES)

    for hh in heads:
        for j in range(nq):
            vt_ref[hh, j] = v_ref[j * tq:(j + 1) * tq, head_lanes(hh)].astype(F32).T.astype(BF16)
    for ref in (s1_ref, mx1_ref, acc_ref, st_ref):
        ref[...] = jnp.zeros_like(ref)

    decs = [dec_ref[hg * ATT_HEADS_PER_STEP + hh] for hh in heads]
    lo_map = lax.broadcasted_iota(jnp.int32, (tq, LANES), 1) < DA_HEAD_DIM
    lv = lamv_ref[...]
    lam = (jnp.exp(jnp.sum(lv[0:1] * lv[1:2], axis=-1, keepdims=True))
           - jnp.exp(jnp.sum(lv[2:3] * lv[3:4], axis=-1, keepdims=True)) + lam_init)
    out_gain = g_ref[...] * (1.0 - lam_init)

    def stages(hh, slot, pair_a, pair_b):
        (qa, ka), (qb, kb) = pair_a, pair_b
        s_a, s_b = s_refs[slot].at[hh], s_refs[1 - slot].at[hh]
        mx_a, mx_b = mx_refs[slot].at[hh], mx_refs[1 - slot].at[hh]
        p_h, st_h, acc_h = p_ref.at[hh], st_ref.at[hh], acc_ref.at[hh]
        lanes = head_lanes(hh)

        q = q_ref[pl.ds(pl.multiple_of(qa * tq, tq), tq), lanes]
        kt = k_ref[pl.ds(pl.multiple_of(ka * tq, tq), tq), lanes]
        zero = jnp.zeros_like(q)
        bias = bias_ref[hh, (ka == qa).astype(jnp.int32)]
        for mi, qz in enumerate((jnp.where(lo_map, q, zero), jnp.where(lo_map, zero, q))):
            s = _dot_nt(kt, qz) + bias
            s_a[mi] = s
            mx_a[mi] = _fold8(s, jnp.maximum)

        first = kb == 0
        vt = vt_ref[hh, kb]
        for mi in range(2):
            m_old = jnp.where(first, MASK_VALUE, st_h[mi, 0:1, :])
            l_old = jnp.where(first, 0.0, st_h[mi, 1:2, :])
            mp = m_old - decs[hh]
            mn = jnp.maximum(mp, jnp.max(mx_b[mi], axis=0, keepdims=True))
            alpha = jnp.exp2(mp - mn)
            l8 = None
            for c in range(0, tq, ATT_ROWS):
                p = jnp.exp2(s_b[mi, c:c + ATT_ROWS, :] - mn)
                p_h[mi, c:c + ATT_ROWS, :] = p.astype(BF16)
                cs = _fold8(p, jnp.add)
                l8 = cs if l8 is None else l8 + cs
            st_h[mi, 0:1, :] = mn
            st_h[mi, 1:2, :] = alpha * l_old + jnp.sum(l8, axis=0, keepdims=True)
            kept = jnp.where(first, 0.0, alpha * acc_h[mi])
            acc_h[mi] = kept + _dot(vt, p_h[mi])

    def step(slot, pair_a, pair_b):
        for hh in heads:
            stages(hh, slot, pair_a, pair_b)
        qb, kb = pair_b

        @pl.when(kb == qb)
        def _():
            for hh in heads:
                st_h, acc_h = st_ref.at[hh], acc_ref.at[hh]
                ot = acc_h[0] / st_h[0, 1:2, :] - lam * (acc_h[1] / st_h[1, 1:2, :])
                ot = ot * lax.rsqrt(jnp.mean(ot * ot, axis=0, keepdims=True) + EPS) * out_gain
                o_ref[pl.ds(pl.multiple_of(qb * tq, tq), tq), head_lanes(hh)] = ot.T.astype(BF16)

    def advance(pair):
        qx, kx = pair
        row_done = kx == qx
        last = jnp.logical_and(row_done, qx == nq - 1)
        return (jnp.where(jnp.logical_and(row_done, jnp.logical_not(last)), qx + 1, qx),
                jnp.where(last, kx, jnp.where(row_done, 0, kx + 1)))

    def body(_, pairs):
        pa, pb = pairs
        step(0, pa, pb)
        pa2 = advance(pa)
        step(1, pa2, pa)
        return (advance(pa2), pa2)

    origin = (jnp.int32(0), jnp.int32(0))
    pa, pb = lax.fori_loop(0, n_pairs // 2, body, (origin, origin))
    step(0, pa, pb)


def _alibi_tiles(tq):
    slopes = 2.0 ** (-8.0 * np.arange(1, DA_HEADS + 1) / DA_HEADS)
    kpos = np.arange(tq)[:, None]
    qpos = np.arange(tq)[None, :]
    low = slopes[:, None, None] * LOG2E * (kpos - qpos)[None]
    diag = np.where((kpos // CHUNK <= qpos // CHUNK)[None],
                    -slopes[:, None, None] * LOG2E * np.abs(qpos - kpos)[None], MASK_VALUE)
    dec = slopes * LOG2E * tq
    return jnp.asarray(dec, F32), jnp.asarray(np.stack([low, diag], axis=1), F32)


def _diff_attention(proj, lamv, subln_g, *, batch, seq, tq, lam_init):
    t = proj.shape[0]
    nq = seq // tq
    dec, bias = _alibi_tiles(tq)
    hp = ATT_HEADS_PER_STEP
    assert DA_HEADS % hp == 0
    groups = DA_HEADS // hp
    wide = hp * LANES
    grid_spec = pltpu.PrefetchScalarGridSpec(
        num_scalar_prefetch=1,
        grid=(batch, groups),
        in_specs=[
            pl.BlockSpec((seq, wide), lambda b, h, dec: (b, h)),
            pl.BlockSpec((seq, wide), lambda b, h, dec: (b, groups + h)),
            pl.BlockSpec((seq, wide), lambda b, h, dec: (b, 2 * groups + h)),
            pl.BlockSpec((hp, 2, tq, tq), lambda b, h, dec: (h, 0, 0, 0)),
            pl.BlockSpec((4, DA_HEAD_DIM), lambda b, h, dec: (0, 0)),
            pl.BlockSpec((LANES, 1), lambda b, h, dec: (0, 0)),
        ],
        out_specs=pl.BlockSpec((seq, wide), lambda b, h, dec: (b, h)),
        scratch_shapes=[
            pltpu.VMEM((hp, nq, LANES, tq), BF16),
            pltpu.VMEM((hp, 2, tq, tq), F32),
            pltpu.VMEM((hp, 2, tq, tq), F32),
            pltpu.VMEM((hp, 2, 8, tq), F32),
            pltpu.VMEM((hp, 2, 8, tq), F32),
            pltpu.VMEM((hp, 2, tq, tq), BF16),
            pltpu.VMEM((hp, 2, 8, tq), F32),
            pltpu.VMEM((hp, 2, LANES, tq), F32),
        ],
    )
    return pl.pallas_call(
        functools.partial(_attn_kernel, tq=tq, lam_init=lam_init),
        out_shape=jax.ShapeDtypeStruct((t, DA_WIDTH), BF16),
        grid_spec=grid_spec,
        compiler_params=_params(("parallel", "parallel")),
        name="diff_attn",
    )(dec, proj, proj, proj, bias, lamv, subln_g.reshape(LANES, 1))


HG_HEADS_PER_STEP = 3


def _hgrn_kernel(q_ref, fz_ref, v_ref, gate_ref, lbl_ref, g_ref, o_ref, st_ref, *, layer, n_chunks):
    @pl.when(pl.program_id(2) == 0)
    def _():
        st_ref[...] = jnp.zeros_like(st_ref)

    rr = lax.broadcasted_iota(jnp.int32, (CHUNK, CHUNK), 0)
    cc = lax.broadcasted_iota(jnp.int32, (CHUNK, CHUNK), 1)
    tril = (cc <= rr).astype(F32)
    sub_row = lax.broadcasted_iota(jnp.int32, (SUB_BLOCK, LANES), 0)
    n_sub = CHUNK // SUB_BLOCK
    gain = g_ref[...]
    r2 = lax.broadcasted_iota(jnp.int32, (2 * CHUNK, CHUNK), 0)
    c2 = lax.broadcasted_iota(jnp.int32, (2 * CHUNK, CHUNK), 1)
    in_chunk = jnp.logical_and(r2 < CHUNK, c2 <= r2)
    in_sub = jnp.logical_and(jnp.logical_and(r2 >= CHUNK, c2 <= r2 - CHUNK),
                             c2 // SUB_BLOCK == (r2 - CHUNK) // SUB_BLOCK)
    cum_w = jnp.where(jnp.logical_or(in_chunk, in_sub), 1.0, 0.0).astype(BF16)
    blk_r, blk_c = rr // SUB_BLOCK, cc // SUB_BLOCK
    mask_sub = jnp.logical_and(blk_r == blk_c, cc <= rr)
    mask_16 = jnp.logical_and(blk_r % 2 == 1, blk_c == blk_r - 1)
    zeros16 = jnp.zeros((SUB_BLOCK, LANES), F32)
    zeros32 = jnp.zeros((2 * SUB_BLOCK, LANES), F32)

    def head_lanes(hh):
        return slice(hh * LANES, (hh + 1) * LANES)

    def lower_bound(hh):
        lbl = lbl_ref[:, head_lanes(hh)]
        e = jnp.exp(lbl - jnp.max(lbl, axis=0, keepdims=True))
        pr = e / jnp.sum(e, axis=0, keepdims=True)
        return jnp.sum(pr[0:layer + 1], axis=0, keepdims=True) - pr[0:1]

    def prepare(hh):
        lb = lower_bound(hh)
        z_all = fz_ref[:, head_lanes(hh)]
        f_all = lb + (1.0 - lb) * (1.0 / (1.0 + jnp.exp(-z_all)))
        k_all = (1.0 - lb) * (1.0 / (1.0 + jnp.exp(z_all)))
        logf = jnp.log(f_all)
        lf_hi = logf.astype(BF16)
        lf_r1 = logf - lf_hi.astype(F32)
        lf_mid = lf_r1.astype(BF16)
        lf_lo = (lf_r1 - lf_mid.astype(F32)).astype(BF16)
        lf3 = jnp.concatenate([lf_hi, lf_mid, lf_lo], axis=1)
        cums = []
        for c in range(n_chunks):
            e3 = _dot(cum_w, lf3[c * CHUNK:(c + 1) * CHUNK])
            cums.append(e3[:, 0:LANES] + e3[:, LANES:2 * LANES] + e3[:, 2 * LANES:3 * LANES])
        min_rel = cums[0][CHUNK:]
        for c in range(1, n_chunks):
            min_rel = jnp.minimum(min_rel, cums[c][CHUNK:])
        return k_all, cums, jnp.min(min_rel)

    prepared = [prepare(hh) for hh in range(HG_HEADS_PER_STEP)]
    worst = prepared[0][2]
    for _, _, m in prepared[1:]:
        worst = jnp.minimum(worst, m)
    safe = worst >= HG_SAFE_LOG_DECAY

    def fast_chunk(hh, c, st):
        k_all, cums, _ = prepared[hh]
        lanes = head_lanes(hh)
        rows = slice(c * CHUNK, (c + 1) * CHUNK)
        q = q_ref[rows, lanes].astype(F32)
        vb = v_ref[rows, lanes]
        k = k_all[rows]
        b, rel = cums[c][:CHUNK], cums[c][CHUNK:]
        b15, b31, b47, b63 = b[15:16], b[31:32], b[47:48], b[63:64]
        q32 = jnp.concatenate([zeros32, q[32:64] * jnp.exp(b[32:64] - b31)], axis=0)
        k32 = jnp.concatenate([k[0:32] * jnp.exp(b31 - b[0:32]), zeros32], axis=0)
        q16 = jnp.concatenate([zeros16, q[16:32] * jnp.exp(b[16:32] - b15),
                               zeros16, q[48:64] * jnp.exp(b[48:64] - b47)], axis=0)
        k16 = jnp.concatenate([k[0:16] * jnp.exp(b15 - b[0:16]), zeros16,
                               k[32:48] * jnp.exp(b47 - b[32:48]), zeros16], axis=0)
        attn = (_dot_nt(q32.astype(BF16), k32.astype(BF16))
                + jnp.where(mask_16, _dot_nt(q16.astype(BF16), k16.astype(BF16)), 0.0)
                + jnp.where(mask_sub, _dot_nt((q * jnp.exp(rel)).astype(BF16),
                                              (k * jnp.exp(-rel)).astype(BF16)), 0.0))
        o = _dot(attn.astype(BF16), vb) + _dot_nt((q * jnp.exp(b)).astype(BF16), st.astype(BF16))
        upd = lax.dot_general(vb, (k * jnp.exp(b63 - b)).astype(BF16), (((0,), (0,)), ((), ())),
                              preferred_element_type=F32)
        gt = gate_ref[rows, lanes].astype(F32)
        o_ref[rows, lanes] = (_rms(o, gain) * (gt * (1.0 / (1.0 + jnp.exp(-gt))))).astype(BF16)
        return st * jnp.exp(b63) + upd

    @pl.when(safe)
    def _():
        states = [st_ref[hh] for hh in range(HG_HEADS_PER_STEP)]
        for c in range(n_chunks):
            states = [fast_chunk(hh, c, st) for hh, st in enumerate(states)]
        for hh, st in enumerate(states):
            st_ref[hh] = st

    def exact_chunk(hh, lb, c):
        lanes = head_lanes(hh)
        rows = pl.ds(pl.multiple_of(c * CHUNK, CHUNK), CHUNK)
        z = fz_ref[rows, lanes]
        q = q_ref[rows, lanes].astype(F32)
        v = v_ref[rows, lanes].astype(F32)
        f = lb + (1.0 - lb) * (1.0 / (1.0 + jnp.exp(-z)))
        k = (1.0 - lb) * (1.0 / (1.0 + jnp.exp(z)))
        b = jnp.dot(tril, jnp.log(f), preferred_element_type=F32, precision=lax.Precision.HIGHEST)
        vb = v.astype(BF16)
        st = st_ref[hh]
        o_inter = _dot_nt((q * jnp.exp(b)).astype(BF16), st.astype(BF16))

        outs = []
        for i in range(n_sub):
            r0 = i * SUB_BLOCK
            qi = q[r0:r0 + SUB_BLOCK]
            ki = k[r0:r0 + SUB_BLOCK]
            bi = b[r0:r0 + SUB_BLOCK]
            vi = v[r0:r0 + SUB_BLOCK]
            oi = jnp.zeros((SUB_BLOCK, LANES), F32)
            for s in range(SUB_BLOCK):
                dcy = jnp.where(sub_row >= s, jnp.exp(bi - bi[s:s + 1]), 0.0)
                a_s = jnp.sum(dcy * qi * ki[s:s + 1], axis=-1, keepdims=True)
                oi = oi + a_s * vi[s:s + 1]
            if i > 0:
                ref_b = b[r0 - 1:r0]
                qd = (qi * jnp.exp(bi - ref_b)).astype(BF16)
                kd = (k[0:r0] * jnp.exp(ref_b - b[0:r0])).astype(BF16)
                oi = oi + _dot(_dot_nt(qd, kd).astype(BF16), vb[0:r0])
            outs.append(oi)
        o = jnp.concatenate(outs, axis=0) + o_inter

        b_last = b[CHUNK - 1:CHUNK]
        kd = (k * jnp.exp(b_last - b)).astype(BF16)
        upd = lax.dot_general(vb, kd, (((0,), (0,)), ((), ())), preferred_element_type=F32)
        st_ref[hh] = st * jnp.exp(b_last) + upd

        gt = gate_ref[rows, lanes].astype(F32)
        o_ref[rows, lanes] = (_rms(o, gain) * (gt * (1.0 / (1.0 + jnp.exp(-gt))))).astype(BF16)

    @pl.when(jnp.logical_not(safe))
    def _():
        for hh in range(HG_HEADS_PER_STEP):
            lb = lower_bound(hh)
            lax.fori_loop(0, n_chunks, lambda c, carry, hh=hh, lb=lb: (exact_chunk(hh, lb, c), carry)[1], 0)


def _hgrn(proj, fz32, lb_logits, out_norm_g, *, batch, seq, ts, layer):
    t = proj.shape[0]
    ns = seq // ts
    n_b = lb_logits.shape[0]
    hp = HG_HEADS_PER_STEP
    assert HG_HEADS % hp == 0
    groups = HG_HEADS // hp
    wide = hp * LANES
    return pl.pallas_call(
        functools.partial(_hgrn_kernel, layer=layer, n_chunks=ts // CHUNK),
        out_shape=jax.ShapeDtypeStruct((t, HG_VAL_WIDTH), BF16),
        grid=(batch, groups, ns),
        in_specs=[
            pl.BlockSpec((ts, wide), lambda b, h, i: (b * ns + i, h)),
            pl.BlockSpec((ts, wide), lambda b, h, i: (b * ns + i, h)),
            pl.BlockSpec((ts, wide), lambda b, h, i: (b * ns + i, 2 * groups + h)),
            pl.BlockSpec((ts, wide), lambda b, h, i: (b * ns + i, 3 * groups + h)),
            pl.BlockSpec((n_b, wide), lambda b, h, i: (0, h)),
            pl.BlockSpec((1, LANES), lambda b, h, i: (0, 0)),
        ],
        out_specs=pl.BlockSpec((ts, wide), lambda b, h, i: (b * ns + i, h)),
        scratch_shapes=[pltpu.VMEM((hp, HG_VAL_DIM, HG_KEY_DIM), F32)],
        compiler_params=_params(("parallel", "parallel", "arbitrary")),
        name="hgrn2",
    )(proj, fz32, proj, proj, lb_logits, out_norm_g.reshape(1, LANES))


def _outproj_kernel(o_ref, qm_ref, kv_ref, x_ref, w_ref, out_ref):
    qm = qm_ref[...]
    kmem = kv_ref[0, :, 0:MEM_WIDTH]
    vmem = kv_ref[0, :, MEM_WIDTH:2 * MEM_WIDTH]
    head = lax.broadcasted_iota(jnp.int32, qm.shape, 1) // MEM_HEAD_DIM
    om = jnp.zeros(qm.shape, F32)
    for hh in range(MEM_HEADS):
        qh = jnp.where(head == hh, qm, jnp.zeros_like(qm))
        s = _dot_nt(qh, kmem) * (MEM_HEAD_DIM ** -0.5)
        p = jnp.exp(s - jnp.max(s, axis=-1, keepdims=True))
        p = p / jnp.sum(p, axis=-1, keepdims=True)
        om = jnp.where(head == hh, _dot(p.astype(BF16), vmem), om)
    wo = o_ref.shape[1]
    out_ref[...] = (x_ref[...] + _dot(o_ref[...], w_ref[0:wo, :])
                    + _dot(om.astype(BF16), w_ref[wo:wo + MEM_WIDTH, :]))


def _outproj(o, proj, qm_block, memkv, layer, x2d, w_out_bf16, *, tm, seq, mem_len):
    t, d = x2d.shape
    wo = o.shape[1]
    tiles_per_batch = seq // tm
    return pl.pallas_call(
        _outproj_kernel,
        out_shape=jax.ShapeDtypeStruct((t, d), F32),
        grid=(t // tm,),
        in_specs=[
            pl.BlockSpec((tm, wo), lambda i: (i, 0)),
            pl.BlockSpec((tm, MEM_WIDTH), lambda i: (i, qm_block)),
            pl.BlockSpec((1, mem_len, 2 * MEM_WIDTH), lambda i: (layer, i // tiles_per_batch, 0)),
            pl.BlockSpec((tm, d), lambda i: (i, 0)),
            pl.BlockSpec((wo + MEM_WIDTH, d), lambda i: (0, 0)),
        ],
        out_specs=pl.BlockSpec((tm, d), lambda i: (i, 0)),
        compiler_params=_params(("parallel",)),
        name="outproj",
    )(o, proj, memkv, x2d, w_out_bf16)


MXU_COLS = 256


def _ffn_kernel(te_ref, nt_ref, x_ref, g_ref, wg_ref, wu_ref, wd_ref, out_ref, *, dense):
    del te_ref
    i = pl.program_id(0)
    f = wd_ref.shape[1]
    half = f // 2 // MXU_COLS * MXU_COLS

    @pl.when(i < nt_ref[0])
    def _():
        x = x_ref[...]
        xn = (_rms(x, g_ref[...]) if dense else x).astype(BF16)
        y = x if dense else None
        for c0, c1 in ((0, half), (half, f)):
            gt = _dot(xn, wg_ref[0, :, c0:c1])
            up = _dot(xn, wu_ref[0, :, c0:c1])
            act = (gt * (1.0 / (1.0 + jnp.exp(-gt))) * up).astype(BF16)
            part = _dot(act, wd_ref[0, c0:c1, :])
            y = part if y is None else y + part
        out_ref[...] = y

    @pl.when(i >= nt_ref[0])
    def _():
        out_ref[...] = jnp.zeros_like(out_ref)


def _ffn(tile_expert, n_tiles, x2d, g, w_gate_up_bf16, w_down_bf16, *, tm, dense):
    r, d = x2d.shape
    f = w_down_bf16.shape[1]
    resident = pl.Buffered(1)
    grid_spec = pltpu.PrefetchScalarGridSpec(
        num_scalar_prefetch=2,
        grid=(r // tm,),
        in_specs=[
            pl.BlockSpec((tm, d), lambda i, te, nt: (jnp.minimum(i, nt[0] - 1), 0)),
            pl.BlockSpec((1, d), lambda i, te, nt: (0, 0)),
            pl.BlockSpec((1, d, f), lambda i, te, nt: (te[i], 0, 0), pipeline_mode=resident),
            pl.BlockSpec((1, d, f), lambda i, te, nt: (te[i], 0, 1), pipeline_mode=resident),
            pl.BlockSpec((1, f, d), lambda i, te, nt: (te[i], 0, 0), pipeline_mode=resident),
        ],
        out_specs=pl.BlockSpec((tm, d), lambda i, te, nt: (i, 0)),
    )
    return pl.pallas_call(
        functools.partial(_ffn_kernel, dense=dense),
        out_shape=jax.ShapeDtypeStruct((r, d), F32),
        grid_spec=grid_spec,
        compiler_params=_params(("parallel",)),
        name="swiglu_dense" if dense else "swiglu_grouped",
    )(tile_expert, n_tiles, x2d, g.reshape(1, d), w_gate_up_bf16, w_gate_up_bf16, w_down_bf16)


META_E1, META_E2, META_W1, META_W2, META_R1, META_R2 = range(6)


def _router_kernel(x_ref, g_ref, wr_ref, meta_ref, cnt_ref, run_ref):
    @pl.when(pl.program_id(0) == 0)
    def _():
        run_ref[...] = jnp.zeros_like(run_ref)

    tm = x_ref.shape[0]
    xn = _rms(x_ref[...], g_ref[...])
    x_hi = xn.astype(BF16)
    x_lo = (xn - x_hi.astype(F32)).astype(BF16)
    w = wr_ref[...]
    w_hi = w.astype(BF16)
    w_lo = (w - w_hi.astype(F32)).astype(BF16)
    logits = _dot(x_hi, w_hi) + (_dot(x_lo, w_hi) + _dot(x_hi, w_lo))
    lane = lax.broadcasted_iota(jnp.int32, logits.shape, 1)
    logits = jnp.where(lane < N_EXPERTS, logits, -jnp.inf)

    def top(vals):
        best = jnp.max(vals, axis=-1, keepdims=True)
        idx = jnp.min(jnp.where(vals == best, lane, LANES), axis=-1, keepdims=True)
        return best, idx

    v1, e1 = top(logits)
    v2, e2 = top(jnp.where(lane == e1, -jnp.inf, logits))
    ex = jnp.exp(v2 - v1)
    w1 = 1.0 / (1.0 + ex)
    w2 = ex / (1.0 + ex)

    oh1 = (lane == e1).astype(F32)
    oh2 = (lane == e2).astype(F32)
    rr = lax.broadcasted_iota(jnp.int32, (tm, tm), 0)
    cc = lax.broadcasted_iota(jnp.int32, (tm, tm), 1)
    strict = (cc < rr).astype(BF16)
    before = _dot(strict, (oh1 + oh2).astype(BF16)) + run_ref[0:1, :]
    r1 = jnp.sum(before * oh1, axis=-1, keepdims=True)
    r2 = jnp.sum(before * oh2, axis=-1, keepdims=True)
    run_ref[0:1, :] = run_ref[0:1, :] + jnp.sum(oh1 + oh2, axis=0, keepdims=True)

    meta = jnp.zeros(logits.shape, F32)
    for slot, val in ((META_E1, e1.astype(F32)), (META_E2, e2.astype(F32)), (META_W1, w1), (META_W2, w2),
                      (META_R1, r1), (META_R2, r2)):
        meta = jnp.where(lane == slot, val, meta)
    meta_ref[...] = meta
    cnt_ref[...] = run_ref[...]


def _router(x2d, g, w_router_padded, *, tm):
    t, d = x2d.shape
    return pl.pallas_call(
        _router_kernel,
        out_shape=[jax.ShapeDtypeStruct((t, LANES), F32), jax.ShapeDtypeStruct((8, LANES), F32)],
        grid=(t // tm,),
        in_specs=[
            pl.BlockSpec((tm, d), lambda i: (i, 0)),
            pl.BlockSpec((1, d), lambda i: (0, 0)),
            pl.BlockSpec((d, LANES), lambda i: (0, 0)),
        ],
        out_specs=[pl.BlockSpec((tm, LANES), lambda i: (i, 0)), pl.BlockSpec((8, LANES), lambda i: (0, 0))],
        scratch_shapes=[pltpu.VMEM((8, LANES), F32)],
        compiler_params=_params(("arbitrary",)),
        name="moe_router",
    )(x2d, g.reshape(1, d), w_router_padded)


ROW_DMA_UNROLL = 8


def _dispatch_kernel(slot_ref, pad_ref, x_ref, g_ref, xs_ref, xn_ref, zrow_ref, sem):
    tm = x_ref.shape[0]

    @pl.when(pl.program_id(0) == 0)
    def _():
        zrow_ref[...] = jnp.zeros_like(zrow_ref)

        def zero_copy(r):
            return pltpu.make_async_copy(zrow_ref.at[pl.ds(0, 1)], xs_ref.at[pl.ds(r, 1)], sem)

        def zero_block(r8):
            rows = pl.ds(pl.multiple_of(r8 * 8, 8), 8)
            return pltpu.make_async_copy(zrow_ref, xs_ref.at[rows], sem)

        tail = (pad_ref[0, 2 * N_EXPERTS - 1] // 8, xs_ref.shape[0] // 8)
        for e in range(N_EXPERTS):
            lo, hi = pad_ref[0, e], pad_ref[0, N_EXPERTS + e]
            lax.fori_loop(lo, hi, lambda r, c: (zero_copy(r).start(), c)[1], 0)
        lax.fori_loop(tail[0], tail[1], lambda r8, c: (zero_block(r8).start(), c)[1], 0)
        for e in range(N_EXPERTS):
            lo, hi = pad_ref[0, e], pad_ref[0, N_EXPERTS + e]
            lax.fori_loop(lo, hi, lambda r, c: (zero_copy(r).wait(), c)[1], 0)
        lax.fori_loop(tail[0], tail[1], lambda r8, c: (zero_block(r8).wait(), c)[1], 0)

    xn_ref[...] = _rms(x_ref[...], g_ref[...])

    def row_copy(r, k):
        return pltpu.make_async_copy(xn_ref.at[pl.ds(r, 1)], xs_ref.at[pl.ds(slot_ref[0, 0, 2 * r + k], 1)], sem)

    def start(r, _):
        row_copy(r, 0).start()
        row_copy(r, 1).start()
        return 0

    def wait(r, _):
        row_copy(r, 0).wait()
        row_copy(r, 1).wait()
        return 0

    lax.fori_loop(0, tm, start, 0, unroll=ROW_DMA_UNROLL)
    lax.fori_loop(0, tm, wait, 0, unroll=ROW_DMA_UNROLL)


def _dispatch(slots3, pad_rows, x2d, g, n_slots, *, tm):
    t, d = x2d.shape
    return pl.pallas_call(
        _dispatch_kernel,
        out_shape=jax.ShapeDtypeStruct((n_slots, d), F32),
        grid=(t // tm,),
        in_specs=[
            pl.BlockSpec((1, 1, 2 * tm), lambda i: (i, 0, 0), memory_space=pltpu.SMEM),
            pl.BlockSpec((1, 2 * N_EXPERTS), lambda i: (0, 0), memory_space=pltpu.SMEM),
            pl.BlockSpec((tm, d), lambda i: (i, 0)),
            pl.BlockSpec((1, d), lambda i: (0, 0)),
        ],
        out_specs=pl.BlockSpec(memory_space=pl.ANY),
        scratch_shapes=[pltpu.VMEM((tm, d), F32), pltpu.VMEM((8, d), F32), pltpu.SemaphoreType.DMA],
        compiler_params=_params(("arbitrary",)),
        name="moe_dispatch",
    )(slots3, pad_rows, x2d, g.reshape(1, d))


def _combine_kernel(slot_ref, x_ref, meta_ref, y_ref, g_ref, out_ref, ybuf_ref, sem, *, final_norm):
    tm = x_ref.shape[0]

    def row_copy(r, k):
        return pltpu.make_async_copy(y_ref.at[pl.ds(slot_ref[0, 0, 2 * r + k], 1)], ybuf_ref.at[k, pl.ds(r, 1)], sem)

    def start(r, _):
        row_copy(r, 0).start()
        row_copy(r, 1).start()
        return 0

    def wait(r, _):
        row_copy(r, 0).wait()
        row_copy(r, 1).wait()
        return 0

    lax.fori_loop(0, tm, start, 0, unroll=ROW_DMA_UNROLL)
    lax.fori_loop(0, tm, wait, 0, unroll=ROW_DMA_UNROLL)
    meta = meta_ref[...]
    w1 = meta[:, META_W1:META_W1 + 1]
    w2 = meta[:, META_W2:META_W2 + 1]
    out = x_ref[...] + (w1 * ybuf_ref[0] + w2 * ybuf_ref[1])
    out_ref[...] = _rms(out, g_ref[...]) if final_norm else out


def _combine(slots3, x2d, meta, y, g, *, tm, final_norm):
    t, d = x2d.shape
    return pl.pallas_call(
        functools.partial(_combine_kernel, final_norm=final_norm),
        out_shape=jax.ShapeDtypeStruct((t, d), F32),
        grid=(t // tm,),
        in_specs=[
            pl.BlockSpec((1, 1, 2 * tm), lambda i: (i, 0, 0), memory_space=pltpu.SMEM),
            pl.BlockSpec((tm, d), lambda i: (i, 0)),
            pl.BlockSpec((tm, LANES), lambda i: (i, 0)),
            pl.BlockSpec(memory_space=pl.ANY),
            pl.BlockSpec((1, d), lambda i: (0, 0)),
        ],
        out_specs=pl.BlockSpec((tm, d), lambda i: (i, 0)),
        scratch_shapes=[pltpu.VMEM((2, tm, d), F32), pltpu.SemaphoreType.DMA],
        compiler_params=_params(("arbitrary",)),
        name="moe_combine",
    )(slots3, x2d, meta, y, g.reshape(1, d))


def _moe(x2d, norm_g, w_router, w_gate_up_bf16, w_down_bf16, first_expert, final_g,
         *, tm_route, tm_row, tm_ffn, final_norm):
    t, d = x2d.shape
    wr = jnp.zeros((d, LANES), F32).at[:, :N_EXPERTS].set(w_router)
    meta, counts = _router(x2d, norm_g, wr, tm=tm_route)

    cnt = counts[0, :N_EXPERTS].astype(jnp.int32)
    padded = (cnt + tm_ffn - 1) // tm_ffn * tm_ffn
    ends = jnp.cumsum(padded)
    starts = ends - padded
    e12 = meta[:, META_E1:META_E2 + 1].astype(jnp.int32)
    r12 = meta[:, META_R1:META_R2 + 1].astype(jnp.int32)
    slots = starts[e12] + r12
    slots3 = slots.reshape(t // tm_row, 1, 2 * tm_row)
    n_tiles_max = (2 * t) // tm_ffn + N_EXPERTS
    tile_start = jnp.arange(n_tiles_max, dtype=jnp.int32) * tm_ffn
    tile_expert = jnp.minimum(
        jnp.sum((ends[None, :] <= tile_start[:, None]).astype(jnp.int32), axis=1), N_EXPERTS - 1)
    n_tiles = (ends[N_EXPERTS - 1] // tm_ffn).astype(jnp.int32).reshape(1)

    pad_rows = jnp.concatenate([starts + cnt, ends]).astype(jnp.int32).reshape(1, 2 * N_EXPERTS)
    xs = _dispatch(slots3, pad_rows, x2d, norm_g, n_tiles_max * tm_ffn, tm=tm_row)
    ys = _ffn(tile_expert + first_expert, n_tiles, xs, norm_g, w_gate_up_bf16, w_down_bf16,
              tm=tm_ffn, dense=False)
    return _combine(slots3, x2d, meta, ys, final_g, tm=tm_row, final_norm=final_norm)


def _tiles(batch, seq):
    full = seq >= 4096
    return dict(
        tm_proj=512 if full else 128,
        tq=512 if full else 128,
        ts=512 if full else 128,
        tm_out=512 if full else 128,
        tm_dense=512 if full else 128,
        tm_route=512 if full else 128,
        tm_row=256 if full else 128,
        tm_ffn=512 if full else 128,
    )


def kernel(x, mem, a_norm_mix, a_w_in, a_lam_q1, a_lam_k1, a_lam_q2, a_lam_k2, a_subln, a_mem_norm, a_w_mem_kv, a_w_out, b_norm_mix, b_w_in, b_lb_logits, b_out_norm, b_mem_norm, b_w_mem_kv, b_w_out, dense_norm, dense_w_gate_up, dense_w_down, moe_norm, moe_router, moe_w_gate_up, moe_w_down, final_norm):
    batch, seq, d = x.shape
    mem_len = mem.shape[1]
    t = batch * seq
    ts = _tiles(batch, seq)
    x2d = x.reshape(t, d)
    mem2d = mem.reshape(batch * mem_len, d)
    kv_a = _memkv(mem2d, a_mem_norm, a_w_mem_kv, mem_len)
    kv_b = _memkv(mem2d, b_mem_norm, b_w_mem_kv, mem_len)
    all_tiles = jnp.full((1,), t // ts["tm_dense"], jnp.int32)
    dense_gu, dense_dn = dense_w_gate_up.astype(BF16), dense_w_down.astype(BF16)
    moe_gu = moe_w_gate_up.astype(BF16).reshape((-1,) + moe_w_gate_up.shape[2:])
    moe_dn = moe_w_down.astype(BF16).reshape((-1,) + moe_w_down.shape[2:])

    for i in range(DEPTH):
        j = i // N_MIXERS
        if i % N_MIXERS == 0:
            lam_init = 0.8 - 0.6 * float(np.exp(-0.3 * i))
            proj = _normproj(x2d, a_norm_mix[j], a_w_in[j].astype(BF16), tm=ts["tm_proj"],
                             scaled_cols=DA_WIDTH, col_scale=DA_HEAD_DIM ** -0.5 * LOG2E)
            lamv = jnp.stack([a_lam_q1[j], a_lam_k1[j], a_lam_q2[j], a_lam_k2[j]])
            o = _diff_attention(proj, lamv, a_subln[j], batch=batch, seq=seq, tq=ts["tq"], lam_init=lam_init)
            x2d = _outproj(o, proj, 3 * DA_WIDTH // MEM_WIDTH, kv_a, j, x2d, a_w_out[j].astype(BF16),
                           tm=ts["tm_out"], seq=seq, mem_len=mem_len)
        else:
            proj, fz32 = _normproj(x2d, b_norm_mix[j], b_w_in[j].astype(BF16), tm=ts["tm_proj"],
                                   f32_cols=(HG_KEY_WIDTH, 2 * HG_KEY_WIDTH))
            o = _hgrn(proj, fz32, b_lb_logits, b_out_norm[j], batch=batch, seq=seq, ts=ts["ts"], layer=j)
            x2d = _outproj(o, proj, (2 * HG_KEY_WIDTH + 2 * HG_VAL_WIDTH) // MEM_WIDTH, kv_b, j, x2d,
                           b_w_out[j].astype(BF16), tm=ts["tm_out"], seq=seq, mem_len=mem_len)
        if i % 2 == 0:
            layer_tiles = jnp.full((t // ts["tm_dense"],), j, jnp.int32)
            x2d = _ffn(layer_tiles, all_tiles, x2d, dense_norm[j], dense_gu, dense_dn, tm=ts["tm_dense"], dense=True)
        else:
            x2d = _moe(x2d, moe_norm[j], moe_router[j], moe_gu, moe_dn, j * N_EXPERTS, final_norm,
                       tm_route=ts["tm_route"], tm_row=ts["tm_row"], tm_ffn=ts["tm_ffn"],
                       final_norm=(i == DEPTH - 1))
    return x2d.reshape(batch, seq, d)
```
